```python
import math
import jax, jax.numpy as jnp
from jax import lax
import numpy as np

D_MODEL = 1024
BATCH = 4
SEQ = 8192
DEPTH = 2

CHUNK = 64
Q_BLOCK = 128
SB_HEADS = 8
SB_HEAD_DIM = 64
SB_WIDTH = SB_HEADS * SB_HEAD_DIM
SSM_WIDTH = D_MODEL // 2
SSM_GROUP = 16
SSM_GROUPS = SSM_WIDTH // SSM_GROUP
SSM_STATE = 64
DT_MIN = 1e-3
DT_MAX = 1e-1
FFN_HIDDEN = ((8 * D_MODEL // 3 + 255) // 256) * 256
IN_SPLITS = (SB_WIDTH, 2 * SB_WIDTH, 3 * SB_WIDTH, 3 * SB_WIDTH + SSM_WIDTH,
             3 * SB_WIDTH + SSM_WIDTH + D_MODEL)
IN_COLS = 3 * SB_WIDTH + SSM_WIDTH + 2 * D_MODEL
N_MOD = 6
DEEPNORM_ALPHA = (2 * DEPTH) ** 0.25
DEEPNORM_BETA = (8 * DEPTH) ** -0.25
LN_EPS = 1e-5

kernel_name = "hybrid_sb_s5_deepnorm_adaln"


def _normalize(x):
    xf = x.astype(jnp.float32)
    mu = jnp.mean(xf, axis=-1, keepdims=True)
    var = jnp.mean(jnp.square(xf - mu), axis=-1, keepdims=True)
    return ((xf - mu) * lax.rsqrt(var + LN_EPS)).astype(x.dtype)


def _layer_norm(x, g, b):
    return _normalize(x) * g + b


def stick_breaking_attention(q, k, v):
    b, s, h, dh = q.shape
    nb = s // Q_BLOCK
    f32 = jnp.float32
    qb = q.astype(f32).reshape(b, nb, Q_BLOCK, h, dh).transpose(1, 0, 3, 2, 4)
    kt = k.astype(f32).transpose(0, 2, 1, 3)
    vt = v.astype(f32).transpose(0, 2, 1, 3)
    key_pos = jnp.arange(s, dtype=jnp.int32)
    scale = 1.0 / math.sqrt(dh)

    def one_block(args):
        q_blk, blk = args
        q_pos = blk * Q_BLOCK + jnp.arange(Q_BLOCK, dtype=jnp.int32)
        z = jnp.einsum('bhqd,bhkd->bhqk', q_blk, kt) * scale
        causal = key_pos[None, :] < q_pos[:, None]
        log_beta = jax.nn.log_sigmoid(z)
        log_one_minus = jnp.where(causal, log_beta - z, 0.0)
        after = lax.cumsum(log_one_minus, axis=3, reverse=True) - log_one_minus
        w = jnp.where(causal, jnp.exp(log_beta + after), 0.0)
        return jnp.einsum('bhqk,bhkd->bhqd', w, vt)

    out = lax.map(one_block, (qb, jnp.arange(nb, dtype=jnp.int32)))
    return out.transpose(1, 0, 3, 2, 4).reshape(b, s, h * dh).astype(v.dtype)


def s5_branch(u, a_re, a_im, log_dt, b_re, b_im, c_re, c_im, d_skip, w_glu, b_glu):
    f32 = jnp.float32
    c64 = jnp.complex64
    bsz, s, _ = u.shape
    n_chunks = s // CHUNK
    uf = u.astype(f32)
    lam = lax.complex(a_re.astype(f32), a_im.astype(f32))
    dt = jnp.exp(log_dt.astype(f32))[:, None]
    lam_dt = lam * dt
    lam_bar = jnp.exp(lam_dt)
    b_mat = lax.complex(b_re.astype(f32), b_im.astype(f32))
    b_bar = ((lam_bar - 1.0) / lam)[..., None] * b_mat
    c_mat = lax.complex(c_re.astype(f32), c_im.astype(f32))
    steps = jnp.arange(1, CHUNK + 1, dtype=f32)
    powers = jnp.exp(lam_dt[None] * steps[:, None, None].astype(c64))
    a_seq = jnp.broadcast_to(lam_bar[None, None], (CHUNK, bsz, SSM_GROUPS, SSM_STATE))

    u_chunks = uf.reshape(bsz, n_chunks, CHUNK, SSM_GROUPS, SSM_GROUP).transpose(1, 2, 0, 3, 4)

    def combine(left, right):
        a_l, b_l = left
        a_r, b_r = right
        return a_r * a_l, a_r * b_l + b_r

    def step(state, u_c):
        bu = jnp.einsum('gpc,lbgc->lbgp', b_bar, u_c.astype(c64))
        _, h_loc = lax.associative_scan(combine, (a_seq, bu), axis=0)
        h = h_loc + powers[:, None] * state[None]
        y = jnp.einsum('gcp,lbgp->lbgc', c_mat, h).real
        return h[-1], y

    state0 = jnp.zeros((bsz, SSM_GROUPS, SSM_STATE), c64)
    _, ys = lax.scan(step, state0, u_chunks)
    y = ys.transpose(2, 0, 1, 3, 4).reshape(bsz, s, SSM_WIDTH)
    y = y + d_skip.astype(f32) * uf
    y = jax.nn.gelu(y)
    y = y * jax.nn.sigmoid(y @ w_glu.astype(f32) + b_glu.astype(f32))
    return y.astype(u.dtype)


def token_mixer(h, w_in, w_sb_up, a_re, a_im, log_dt, b_re, b_im, c_re, c_im,
                d_skip, w_glu, b_glu, w_ssm_up, w_out):
    bsz, s, _ = h.shape
    proj = h @ w_in
    q, k, v, u, g_sb, g_ssm = jnp.split(proj, IN_SPLITS, axis=-1)
    shp = (bsz, s, SB_HEADS, SB_HEAD_DIM)
    y_sb = stick_breaking_attention(q.reshape(shp), k.reshape(shp), v.reshape(shp)) @ w_sb_up
    y_ssm = s5_branch(u, a_re, a_im, log_dt, b_re, b_im, c_re, c_im,
                      d_skip, w_glu, b_glu) @ w_ssm_up
    merged = jax.nn.sigmoid(g_sb) * y_sb + jax.nn.sigmoid(g_ssm) * y_ssm
    return merged @ w_out


def swiglu_ffn(h, w_ffn_in, w_ffn_out):
    gate, up = jnp.split(h @ w_ffn_in, 2, axis=-1)
    return (jax.nn.silu(gate) * up) @ w_ffn_out


def setup_inputs(seed: int = 0) -> dict:
    key = jax.random.key(seed)
    ks = jax.random.split(key, 32)
    f32 = jnp.float32

    def nrm(k, shape, scale):
        return jax.random.normal(k, shape, f32) * scale

    G, P, Cg = SSM_GROUPS, SSM_STATE, SSM_GROUP
    n = jnp.arange(P, dtype=f32)
    return {
        "x": nrm(ks[0], (BATCH, SEQ, D_MODEL), 1.0),
        "c": nrm(ks[1], (BATCH, D_MODEL), 1.0),
        "w_ada": nrm(ks[2], (DEPTH, D_MODEL, N_MOD * D_MODEL), 0.5 * D_MODEL ** -0.5),
        "b_ada": nrm(ks[3], (DEPTH, N_MOD * D_MODEL), 0.02),
        "w_in": nrm(ks[4], (DEPTH, D_MODEL, IN_COLS), D_MODEL ** -0.5),
        "w_sb_up": nrm(ks[5], (DEPTH, SB_WIDTH, D_MODEL), SB_WIDTH ** -0.5),
        "ssm_a_re": -0.5 + nrm(ks[6], (DEPTH, G, P), 0.01),
        "ssm_a_im": math.pi * n + nrm(ks[7], (DEPTH, G, P), 0.01),
        "ssm_log_dt": jax.random.uniform(ks[8], (DEPTH, G), f32,
                                         math.log(DT_MIN), math.log(DT_MAX)),
        "ssm_b_re": nrm(ks[9], (DEPTH, G, P, Cg), (2 * Cg) ** -0.5),
        "ssm_b_im": nrm(ks[10], (DEPTH, G, P, Cg), (2 * Cg) ** -0.5),
        "ssm_c_re": nrm(ks[11], (DEPTH, G, Cg, P), P ** -0.5),
        "ssm_c_im": nrm(ks[12], (DEPTH, G, Cg, P), P ** -0.5),
        "ssm_d": 1.0 + nrm(ks[13], (DEPTH, SSM_WIDTH), 0.1),
        "w_glu": nrm(ks[14], (DEPTH, SSM_WIDTH, SSM_WIDTH), SSM_WIDTH ** -0.5),
        "b_glu": nrm(ks[15], (DEPTH, SSM_WIDTH), 0.02),
        "w_ssm_up": nrm(ks[16], (DEPTH, SSM_WIDTH, D_MODEL), SSM_WIDTH ** -0.5),
        "w_out": nrm(ks[17], (DEPTH, D_MODEL, D_MODEL), D_MODEL ** -0.5 * DEEPNORM_BETA),
        "ln1_g": 1.0 + nrm(ks[18], (DEPTH, D_MODEL), 0.02),
        "ln1_b": nrm(ks[19], (DEPTH, D_MODEL), 0.02),
        "w_ffn_in": nrm(ks[20], (DEPTH, D_MODEL, 2 * FFN_HIDDEN), D_MODEL ** -0.5),
        "w_ffn_out": nrm(ks[21], (DEPTH, FFN_HIDDEN, D_MODEL), FFN_HIDDEN ** -0.5 * DEEPNORM_BETA),
        "ln2_g": 1.0 + nrm(ks[22], (DEPTH, D_MODEL), 0.02),
        "ln2_b": nrm(ks[23], (DEPTH, D_MODEL), 0.02),
    }


def reference(x, c, w_ada, b_ada, w_in, w_sb_up, ssm_a_re, ssm_a_im, ssm_log_dt,
              ssm_b_re, ssm_b_im, ssm_c_re, ssm_c_im, ssm_d, w_glu, b_glu,
              w_ssm_up, w_out, ln1_g, ln1_b, w_ffn_in, w_ffn_out, ln2_g, ln2_b):
    c_act = jax.nn.silu(c)
    for l in range(DEPTH):
        mod = c_act @ w_ada[l] + b_ada[l]
        sh_m, sc_m, g_m, sh_f, sc_f, g_f = [m[:, None, :] for m in jnp.split(mod, N_MOD, axis=-1)]
        h = _normalize(x) * (1.0 + sc_m) + sh_m
        y = token_mixer(h, w_in[l], w_sb_up[l], ssm_a_re[l], ssm_a_im[l], ssm_log_dt[l],
                        ssm_b_re[l], ssm_b_im[l], ssm_c_re[l], ssm_c_im[l], ssm_d[l],
                        w_glu[l], b_glu[l], w_ssm_up[l], w_out[l])
        x = _layer_norm(DEEPNORM_ALPHA * x + (1.0 + g_m) * y, ln1_g[l], ln1_b[l])
        h = _normalize(x) * (1.0 + sc_f) + sh_f
        y = swiglu_ffn(h, w_ffn_in[l], w_ffn_out[l])
        x = _layer_norm(DEEPNORM_ALPHA * x + (1.0 + g_f) * y, ln2_g[l], ln2_b[l])
    return x
```

```python
import functools
import math

import jax
import jax.numpy as jnp
from jax import lax
from jax.experimental import pallas as pl
from jax.experimental.pallas import tpu as pltpu

F32 = jnp.float32
BF16 = jnp.bfloat16

D_MODEL = 1024
SB_HEADS = 8
SB_HEAD_DIM = 64
SB_WIDTH = SB_HEADS * SB_HEAD_DIM
SSM_WIDTH = D_MODEL // 2
SSM_GROUP = 16
SSM_GROUPS = SSM_WIDTH // SSM_GROUP
SSM_STATE = 64
N_STATE = SSM_GROUPS * SSM_STATE
FFN_HIDDEN = 2816
IN_COLS = 3 * SB_WIDTH + SSM_WIDTH + 2 * D_MODEL
N_MOD = 6
LN_EPS = 1e-5

LANES = 128
SUBLANES = 8
VMEM_LIMIT_BYTES = 56 * 1024 * 1024

ROW_TILE = 512
ATT_TILE = 256
SSM_CHUNK = 64
SSM_ROWS = 512
FFN_CHUNK = 256
HEADS_PER_STEP = LANES // SB_HEAD_DIM


def _sigmoid(x):
    return 1.0 / (1.0 + jnp.exp(-x))


def _normalize(x):
    mu = jnp.mean(x, axis=-1, keepdims=True)
    xc = x - mu
    var = jnp.mean(xc * xc, axis=-1, keepdims=True)
    return xc * lax.rsqrt(var + LN_EPS)


def _dot(a, b):
    return jnp.dot(a, b, preferred_element_type=F32)


def _resident(shape):
    zeros = (0,) * len(shape)
    return pl.BlockSpec(shape, lambda *_: zeros, pipeline_mode=pl.Buffered(1))


def _params(*semantics):
    return pltpu.CompilerParams(dimension_semantics=semantics,
                                vmem_limit_bytes=VMEM_LIMIT_BYTES)


def _mod_kernel(c_ref, w_ref, b_ref, o_ref):
    c = c_ref[...]
    c_act = c * _sigmoid(c)
    o_ref[...] = jnp.dot(c_act, w_ref[...], preferred_element_type=F32,
                         precision=lax.Precision.HIGHEST) + b_ref[...]


def _modulation(c, w_ada, b_ada):
    depth = w_ada.shape[0]
    batch = c.shape[0]
    rows = -(-batch // SUBLANES) * SUBLANES
    c_pad = jnp.pad(c, ((0, rows - batch), (0, 0)))
    cols = N_MOD * D_MODEL
    out = pl.pallas_call(
        _mod_kernel,
        grid=(depth, N_MOD),
        in_specs=[
            pl.BlockSpec((rows, D_MODEL), lambda l, j: (0, 0)),
            pl.BlockSpec((None, D_MODEL, D_MODEL), lambda l, j: (l, 0, j)),
            pl.BlockSpec((None, 1, D_MODEL), lambda l, j: (l, 0, j)),
        ],
        out_specs=pl.BlockSpec((None, rows, D_MODEL), lambda l, j: (l, 0, j)),
        out_shape=jax.ShapeDtypeStruct((depth, rows, cols), F32),
        compiler_params=_params("arbitrary", "arbitrary"),
        name="adaln_mod",
    )(c_pad, w_ada, b_ada.reshape(depth, 1, cols))
    return out[:, :batch].reshape(depth, batch, N_MOD, D_MODEL)


def _inproj_kernel(x_ref, mod_ref, w_ref, qkv_ref, u_ref, g_ref, h_ref):
    shift = mod_ref[0:1, :]
    scale = mod_ref[1:2, :]
    h_ref[...] = (_normalize(x_ref[...]) * (1.0 + scale) + shift).astype(BF16)
    h = h_ref[...]
    w = SB_WIDTH
    qkv_ref[:, 0:w] = (_dot(h, w_ref[:, 0:w]) * (1.0 / math.sqrt(SB_HEAD_DIM))).astype(BF16)
    qkv_ref[:, w:2 * w] = _dot(h, w_ref[:, w:2 * w]).astype(BF16)
    qkv_ref[:, 2 * w:3 * w] = _dot(h, w_ref[:, 2 * w:3 * w]).astype(BF16)
    u_ref[...] = _dot(h, w_ref[:, 3 * w:3 * w + SSM_WIDTH])
    g0 = 3 * w + SSM_WIDTH
    for j in range(2 * D_MODEL // w):
        g_ref[:, j * w:(j + 1) * w] = _dot(h, w_ref[:, g0 + j * w:g0 + (j + 1) * w]).astype(BF16)


def _in_projection(x2d, mod, w_in_bf16, seq):
    n = x2d.shape[0]
    tiles_per_seq = seq // ROW_TILE
    return pl.pallas_call(
        _inproj_kernel,
        grid=(n // ROW_TILE,),
        in_specs=[
            pl.BlockSpec((ROW_TILE, D_MODEL), lambda i: (i, 0)),
            pl.BlockSpec((None, N_MOD, D_MODEL), lambda i: (i // tiles_per_seq, 0, 0)),
            _resident((D_MODEL, IN_COLS)),
        ],
        out_specs=[
            pl.BlockSpec((ROW_TILE, 3 * SB_WIDTH), lambda i: (i, 0)),
            pl.BlockSpec((ROW_TILE, SSM_WIDTH), lambda i: (i, 0)),
            pl.BlockSpec((ROW_TILE, 2 * D_MODEL), lambda i: (i, 0)),
        ],
        out_shape=[
            jax.ShapeDtypeStruct((n, 3 * SB_WIDTH), BF16),
            jax.ShapeDtypeStruct((n, SSM_WIDTH), F32),
            jax.ShapeDtypeStruct((n, 2 * D_MODEL), BF16),
        ],
        scratch_shapes=[pltpu.VMEM((ROW_TILE, D_MODEL), BF16)],
        compiler_params=_params("arbitrary"),
        name="in_proj",
    )(x2d, mod, w_in_bf16)


def _attn_kernel(q_ref, k_ref, v_ref, o_ref, acc_ref, carry_ref):
    t = ATT_TILE
    i = pl.program_id(2)
    row = lax.broadcasted_iota(jnp.int32, (t, t), 0)
    col = lax.broadcasted_iota(jnp.int32, (t, t), 1)
    suffix = jnp.where(row > col, 1.0, 0.0).astype(BF16)
    causal = col < row

    for hh in range(HEADS_PER_STEP):
        lanes = slice(hh * SB_HEAD_DIM, (hh + 1) * SB_HEAD_DIM)
        qh = q_ref[:, lanes]

        def block(j, diagonal, lanes=lanes, qh=qh):
            start = pl.multiple_of(j * t, t)
            kj = k_ref[pl.ds(start, t), lanes]
            vj = v_ref[pl.ds(start, t), lanes]
            z = lax.dot_general(qh, kj, (((1,), (1,)), ((), ())),
                                preferred_element_type=F32)
            log_beta = jnp.minimum(z, 0.0) - jnp.log1p(jnp.exp(-jnp.abs(z)))
            lom = log_beta - z
            if diagonal:
                lom = jnp.where(causal, lom, 0.0)
            hi = lom.astype(BF16)
            lo = (lom - hi.astype(F32)).astype(BF16)
            after = _dot(hi, suffix) + _dot(lo, suffix)
            carry = carry_ref[...]
            w = jnp.exp(log_beta + after + carry)
            if diagonal:
                w = jnp.where(causal, w, 0.0)
            acc_ref[...] += _dot(w.astype(BF16), vj)
            carry_ref[...] = carry + after[:, 0:1] + lom[:, 0:1]

        acc_ref[...] = jnp.zeros_like(acc_ref)
        carry_ref[...] = jnp.zeros_like(carry_ref)
        block(i, True)

        def body(n, c):
            block(i - 1 - n, False)
            return c

        lax.fori_loop(0, i, body, 0)
        o_ref[:, lanes] = acc_ref[...].astype(BF16)


def _attention(qkv, batch, seq):
    n = qkv.shape[0]
    t = ATT_TILE
    q_blocks = seq // t
    pairs = SB_HEADS // HEADS_PER_STEP
    return pl.pallas_call(
        _attn_kernel,
        grid=(batch, pairs, q_blocks),
        in_specs=[
            pl.BlockSpec((t, LANES), lambda b, p, i: (b * q_blocks + i, p)),
            pl.BlockSpec((seq, LANES), lambda b, p, i: (b, pairs + p)),
            pl.BlockSpec((seq, LANES), lambda b, p, i: (b, 2 * pairs + p)),
        ],
        out_specs=pl.BlockSpec((t, LANES), lambda b, p, i: (b * q_blocks + i, p)),
        out_shape=jax.ShapeDtypeStruct((n, SB_WIDTH), BF16),
        scratch_shapes=[pltpu.VMEM((t, SB_HEAD_DIM), F32),
                        pltpu.VMEM((t, 1), F32)],
        compiler_params=_params("arbitrary", "arbitrary", "arbitrary"),
        name="sb_attention",
    )(qkv, qkv, qkv)


def _gelu_tanh(x):
    c = math.sqrt(2.0 / math.pi)
    return 0.5 * x * (1.0 + jnp.tanh(c * (x + 0.044715 * (x * x * x))))


def _s5_kernel(u_ref, wb_ref, einv_re_ref, einv_im_ref, e_re_ref, e_im_ref,
               wc_ref, d_ref, wglu_ref, bglu_ref, o_ref, st_re_ref, st_im_ref):
    ell = SSM_CHUNK

    @pl.when(pl.program_id(1) == 0)
    def _():
        st_re_ref[...] = jnp.zeros_like(st_re_ref)
        st_im_ref[...] = jnp.zeros_like(st_im_ref)

    row = lax.broadcasted_iota(jnp.int32, (ell, ell), 0)
    col = lax.broadcasted_iota(jnp.int32, (ell, ell), 1)
    prefix = jnp.where(col <= row, 1.0, 0.0).astype(BF16)

    def chunk(c, carry):
        rows = pl.ds(pl.multiple_of(c * ell, ell), ell)
        u = u_ref[rows, :]
        bu = _dot(u.astype(BF16), wb_ref[...])
        bu_re = bu[:, :N_STATE]
        bu_im = bu[:, N_STATE:]
        einv_re = einv_re_ref[...]
        einv_im = einv_im_ref[...]
        x_re = bu_re * einv_re - bu_im * einv_im
        x_im = bu_re * einv_im + bu_im * einv_re
        a_re = _dot(prefix, x_re.astype(BF16)) + st_re_ref[...]
        a_im = _dot(prefix, x_im.astype(BF16)) + st_im_ref[...]
        e_re = e_re_ref[...]
        e_im = e_im_ref[...]
        h_re = e_re * a_re - e_im * a_im
        h_im = e_re * a_im + e_im * a_re
        st_re_ref[...] = h_re[ell - 1:ell, :]
        st_im_ref[...] = h_im[ell - 1:ell, :]
        y = (_dot(h_re.astype(BF16), wc_ref[:N_STATE, :])
             + _dot(h_im.astype(BF16), wc_ref[N_STATE:, :]))
        y = _gelu_tanh(y + d_ref[...] * u)
        gate = _dot(y.astype(BF16), wglu_ref[...]) + bglu_ref[...]
        o_ref[rows, :] = (y * _sigmoid(gate)).astype(BF16)
        return carry

    lax.fori_loop(0, SSM_ROWS // ell, chunk, 0)


def _s5_tables(a_re, a_im, log_dt, b_re, b_im, c_re, c_im):
    g, p, cg = SSM_GROUPS, SSM_STATE, SSM_GROUP
    lam = lax.complex(a_re, a_im)
    lam_dt = lam * jnp.exp(log_dt)[:, None]
    lam_bar = jnp.exp(lam_dt)
    b_bar = ((lam_bar - 1.0) / lam)[..., None] * lax.complex(b_re, b_im)
    eye = jnp.eye(g, dtype=F32)
    wb_re = jnp.einsum('gpc,gh->gchp', b_bar.real, eye).reshape(g * cg, g * p)
    wb_im = jnp.einsum('gpc,gh->gchp', b_bar.imag, eye).reshape(g * cg, g * p)
    wb = jnp.concatenate([wb_re, wb_im], axis=1).astype(BF16)
    wc_re = jnp.einsum('gcp,gh->gphc', c_re, eye).reshape(g * p, g * cg)
    wc_im = jnp.einsum('gcp,gh->gphc', c_im, eye).reshape(g * p, g * cg)
    wc = jnp.concatenate([wc_re, -wc_im], axis=0).astype(BF16)
    steps = jnp.arange(1, SSM_CHUNK + 1, dtype=F32)[:, None, None]
    e = jnp.exp(lam_dt[None] * steps).reshape(SSM_CHUNK, g * p)
    einv = jnp.exp(-lam_dt[None] * steps).reshape(SSM_CHUNK, g * p)
    return wb, einv.real, einv.imag, e.real, e.imag, wc


def _s5(u, tables, d_skip, w_glu_bf16, b_glu, batch, seq):
    n = u.shape[0]
    wb, einv_re, einv_im, e_re, e_im, wc = tables
    steps = seq // SSM_ROWS
    tab = (SSM_CHUNK, N_STATE)
    return pl.pallas_call(
        _s5_kernel,
        grid=(batch, steps),
        in_specs=[
            pl.BlockSpec((SSM_ROWS, SSM_WIDTH), lambda b, i: (b * steps + i, 0)),
            _resident((SSM_WIDTH, 2 * N_STATE)),
            _resident(tab), _resident(tab), _resident(tab), _resident(tab),
            _resident((2 * N_STATE, SSM_WIDTH)),
            _resident((1, SSM_WIDTH)),
            _resident((SSM_WIDTH, SSM_WIDTH)),
            _resident((1, SSM_WIDTH)),
        ],
        out_specs=pl.BlockSpec((SSM_ROWS, SSM_WIDTH), lambda b, i: (b * steps + i, 0)),
        out_shape=jax.ShapeDtypeStruct((n, SSM_WIDTH), BF16),
        scratch_shapes=[pltpu.VMEM((1, N_STATE), F32), pltpu.VMEM((1, N_STATE), F32)],
        compiler_params=_params("arbitrary", "arbitrary"),
        name="s5_scan",
    )(u, wb, einv_re, einv_im, e_re, e_im, wc,
      d_skip.reshape(1, SSM_WIDTH), w_glu_bf16, b_glu.reshape(1, SSM_WIDTH))


def _mixout_kernel(x_ref, att_ref, s5_ref, g_ref, mod_ref, wsb_ref, wssm_ref,
                   wout_ref, lng_ref, lnb_ref, o_ref, *, alpha):
    y_sb = _dot(att_ref[...], wsb_ref[...])
    y_ssm = _dot(s5_ref[...], wssm_ref[...])
    g_sb = g_ref[:, :D_MODEL].astype(F32)
    g_ssm = g_ref[:, D_MODEL:].astype(F32)
    merged = _sigmoid(g_sb) * y_sb + _sigmoid(g_ssm) * y_ssm
    y = _dot(merged.astype(BF16), wout_ref[...])
    gate = mod_ref[2:3, :]
    r = alpha * x_ref[...] + (1.0 + gate) * y
    o_ref[...] = _normalize(r) * lng_ref[...] + lnb_ref[...]


def _mix_out(x2d, att, s5, g, mod, w_sb_up, w_ssm_up, w_out, ln_g, ln_b, seq, alpha):
    n = x2d.shape[0]
    tiles_per_seq = seq // ROW_TILE
    rows = lambda width: pl.BlockSpec((ROW_TILE, width), lambda i: (i, 0))
    return pl.pallas_call(
        functools.partial(_mixout_kernel, alpha=alpha),
        grid=(n // ROW_TILE,),
        in_specs=[
            rows(D_MODEL), rows(SB_WIDTH), rows(SSM_WIDTH), rows(2 * D_MODEL),
            pl.BlockSpec((None, N_MOD, D_MODEL), lambda i: (i // tiles_per_seq, 0, 0)),
            _resident((SB_WIDTH, D_MODEL)),
            _resident((SSM_WIDTH, D_MODEL)),
            _resident((D_MODEL, D_MODEL)),
            _resident((1, D_MODEL)),
            _resident((1, D_MODEL)),
        ],
        out_specs=rows(D_MODEL),
        out_shape=jax.ShapeDtypeStruct((n, D_MODEL), F32),
        compiler_params=_params("arbitrary"),
        name="mix_out",
    )(x2d, att, s5, g, mod, w_sb_up, w_ssm_up, w_out,
      ln_g.reshape(1, D_MODEL), ln_b.reshape(1, D_MODEL))


def _ffn_kernel(x_ref, mod_ref, win_ref, wout_ref, lng_ref, lnb_ref, o_ref,
                h_ref, acc_ref, *, alpha):
    x = x_ref[...]
    shift = mod_ref[3:4, :]
    scale = mod_ref[4:5, :]
    gate_mod = mod_ref[5:6, :]
    h_ref[...] = (_normalize(x) * (1.0 + scale) + shift).astype(BF16)
    h = h_ref[...]
    for j in range(FFN_HIDDEN // FFN_CHUNK):
        cols = slice(j * FFN_CHUNK, (j + 1) * FFN_CHUNK)
        up_cols = slice(FFN_HIDDEN + j * FFN_CHUNK, FFN_HIDDEN + (j + 1) * FFN_CHUNK)
        gate = _dot(h, win_ref[:, cols])
        up = _dot(h, win_ref[:, up_cols])
        act = (gate * _sigmoid(gate) * up).astype(BF16)
        part = _dot(act, wout_ref[cols, :])
        if j == 0:
            acc_ref[...] = part
        else:
            acc_ref[...] += part
    r = alpha * x + (1.0 + gate_mod) * acc_ref[...]
    o_ref[...] = _normalize(r) * lng_ref[...] + lnb_ref[...]


def _ffn(x2d, mod, w_ffn_in, w_ffn_out, ln_g, ln_b, seq, alpha):
    n = x2d.shape[0]
    tiles_per_seq = seq // ROW_TILE
    return pl.pallas_call(
        functools.partial(_ffn_kernel, alpha=alpha),
        grid=(n // ROW_TILE,),
        in_specs=[
            pl.BlockSpec((ROW_TILE, D_MODEL), lambda i: (i, 0)),
            pl.BlockSpec((None, N_MOD, D_MODEL), lambda i: (i // tiles_per_seq, 0, 0)),
            _resident((D_MODEL, 2 * FFN_HIDDEN)),
            _resident((FFN_HIDDEN, D_MODEL)),
            _resident((1, D_MODEL)),
            _resident((1, D_MODEL)),
        ],
        out_specs=pl.BlockSpec((ROW_TILE, D_MODEL), lambda i: (i, 0)),
        out_shape=jax.ShapeDtypeStruct((n, D_MODEL), F32),
        scratch_shapes=[pltpu.VMEM((ROW_TILE, D_MODEL), BF16),
                        pltpu.VMEM((ROW_TILE, D_MODEL), F32)],
        compiler_params=_params("arbitrary"),
        name="ffn",
    )(x2d, mod, w_ffn_in, w_ffn_out, ln_g.reshape(1, D_MODEL), ln_b.reshape(1, D_MODEL))


def kernel(x, c, w_ada, b_ada, w_in, w_sb_up, ssm_a_re, ssm_a_im, ssm_log_dt,
           ssm_b_re, ssm_b_im, ssm_c_re, ssm_c_im, ssm_d, w_glu, b_glu,
           w_ssm_up, w_out, ln1_g, ln1_b, w_ffn_in, w_ffn_out, ln2_g, ln2_b):
    batch, seq, d = x.shape
    depth = w_in.shape[0]
    assert d == D_MODEL and seq % max(ROW_TILE, ATT_TILE, SSM_ROWS) == 0
    alpha = (2 * depth) ** 0.25
    mod = _modulation(c, w_ada, b_ada)
    x2d = x.reshape(batch * seq, d)
    for l in range(depth):
        qkv, u, g = _in_projection(x2d, mod[l], w_in[l].astype(BF16), seq)
        att = _attention(qkv, batch, seq)
        tables = _s5_tables(ssm_a_re[l], ssm_a_im[l], ssm_log_dt[l], ssm_b_re[l],
                            ssm_b_im[l], ssm_c_re[l], ssm_c_im[l])
        s5 = _s5(u, tables, ssm_d[l], w_glu[l].astype(BF16), b_glu[l], batch, seq)
        x2d = _mix_out(x2d, att, s5, g, mod[l], w_sb_up[l].astype(BF16),
                       w_ssm_up[l].astype(BF16), w_out[l].astype(BF16),
                       ln1_g[l], ln1_b[l], seq, alpha)
        x2d = _ffn(x2d, mod[l], w_ffn_in[l].astype(BF16), w_ffn_out[l].astype(BF16),
                   ln2_g[l], ln2_b[l], seq, alpha)
    return x2d.reshape(batch, seq, d)
```

```python
import functools
import math

import jax
import jax.numpy as jnp
from jax import lax
from jax.experimental import pallas as pl
from jax.experimental.pallas import tpu as pltpu

F32 = jnp.float32
BF16 = jnp.bfloat16

D_MODEL = 1024
SB_HEADS = 8
SB_HEAD_DIM = 64
SB_WIDTH = SB_HEADS * SB_HEAD_DIM
SSM_WIDTH = D_MODEL // 2
SSM_GROUP = 16
SSM_GROUPS = SSM_WIDTH // SSM_GROUP
SSM_STATE = 64
N_STATE = SSM_GROUPS * SSM_STATE
FFN_HIDDEN = 2816
IN_COLS = 3 * SB_WIDTH + SSM_WIDTH + 2 * D_MODEL
N_MOD = 6
LN_EPS = 1e-5

LANES = 128
SUBLANES = 8
VMEM_LIMIT_BYTES = 56 * 1024 * 1024

ROW_TILE = 512
ATT_TILE = 256
SSM_CHUNK = 64
SSM_ROWS = 512
FFN_CHUNK = 256
HEADS_PER_STEP = LANES // SB_HEAD_DIM
SB_EXIT_LOG = -105.0


def _sigmoid(x):
    return 1.0 / (1.0 + jnp.exp(-x))


def _normalize(x):
    mu = jnp.mean(x, axis=-1, keepdims=True)
    xc = x - mu
    var = jnp.mean(xc * xc, axis=-1, keepdims=True)
    return xc * lax.rsqrt(var + LN_EPS)


def _dot(a, b):
    return jnp.dot(a, b, preferred_element_type=F32)


def _resident(shape):
    zeros = (0,) * len(shape)
    return pl.BlockSpec(shape, lambda *_: zeros, pipeline_mode=pl.Buffered(1))


def _params(*semantics):
    return pltpu.CompilerParams(dimension_semantics=semantics,
                                vmem_limit_bytes=VMEM_LIMIT_BYTES)


def _mod_kernel(c_ref, w_ref, b_ref, o_ref):
    c = c_ref[...]
    c_act = c * _sigmoid(c)
    o_ref[...] = jnp.dot(c_act, w_ref[...], preferred_element_type=F32,
                         precision=lax.Precision.HIGHEST) + b_ref[...]


def _modulation(c, w_ada, b_ada):
    depth = w_ada.shape[0]
    batch = c.shape[0]
    rows = -(-batch // SUBLANES) * SUBLANES
    c_pad = jnp.pad(c, ((0, rows - batch), (0, 0)))
    cols = N_MOD * D_MODEL
    out = pl.pallas_call(
        _mod_kernel,
        grid=(depth, N_MOD),
        in_specs=[
            pl.BlockSpec((rows, D_MODEL), lambda l, j: (0, 0)),
            pl.BlockSpec((None, D_MODEL, D_MODEL), lambda l, j: (l, 0, j)),
            pl.BlockSpec((None, 1, D_MODEL), lambda l, j: (l, 0, j)),
        ],
        out_specs=pl.BlockSpec((None, rows, D_MODEL), lambda l, j: (l, 0, j)),
        out_shape=jax.ShapeDtypeStruct((depth, rows, cols), F32),
        compiler_params=_params("arbitrary", "arbitrary"),
        name="adaln_mod",
    )(c_pad, w_ada, b_ada.reshape(depth, 1, cols))
    return out[:, :batch].reshape(depth, batch, N_MOD, D_MODEL)


def _inproj_kernel(x_ref, mod_ref, w_ref, qkv_ref, u_ref, g_ref, h_ref):
    shift = mod_ref[0:1, :]
    scale = mod_ref[1:2, :]
    h_ref[...] = (_normalize(x_ref[...]) * (1.0 + scale) + shift).astype(BF16)
    h = h_ref[...]
    w = SB_WIDTH
    qkv_ref[:, 0:w] = (_dot(h, w_ref[:, 0:w]) * (1.0 / math.sqrt(SB_HEAD_DIM))).astype(BF16)
    qkv_ref[:, w:2 * w] = _dot(h, w_ref[:, w:2 * w]).astype(BF16)
    qkv_ref[:, 2 * w:3 * w] = _dot(h, w_ref[:, 2 * w:3 * w]).astype(BF16)
    u_ref[...] = _dot(h, w_ref[:, 3 * w:3 * w + SSM_WIDTH])
    g0 = 3 * w + SSM_WIDTH
    for j in range(2 * D_MODEL // w):
        g_ref[:, j * w:(j + 1) * w] = _dot(h, w_ref[:, g0 + j * w:g0 + (j + 1) * w]).astype(BF16)


def _in_projection(x2d, mod, w_in_bf16, seq):
    n = x2d.shape[0]
    tiles_per_seq = seq // ROW_TILE
    return pl.pallas_call(
        _inproj_kernel,
        grid=(n // ROW_TILE,),
        in_specs=[
            pl.BlockSpec((ROW_TILE, D_MODEL), lambda i: (i, 0)),
            pl.BlockSpec((None, N_MOD, D_MODEL), lambda i: (i // tiles_per_seq, 0, 0)),
            _resident((D_MODEL, IN_COLS)),
        ],
        out_specs=[
            pl.BlockSpec((ROW_TILE, 3 * SB_WIDTH), lambda i: (i, 0)),
            pl.BlockSpec((ROW_TILE, SSM_WIDTH), lambda i: (i, 0)),
            pl.BlockSpec((ROW_TILE, 2 * D_MODEL), lambda i: (i, 0)),
        ],
        out_shape=[
            jax.ShapeDtypeStruct((n, 3 * SB_WIDTH), BF16),
            jax.ShapeDtypeStruct((n, SSM_WIDTH), F32),
            jax.ShapeDtypeStruct((n, 2 * D_MODEL), BF16),
        ],
        scratch_shapes=[pltpu.VMEM((ROW_TILE, D_MODEL), BF16)],
        compiler_params=_params("arbitrary"),
        name="in_proj",
    )(x2d, mod, w_in_bf16)


def _attn_kernel(q_ref, k_ref, v_ref, o_ref, acc_ref, carry_ref):
    t = ATT_TILE
    i = pl.program_id(2)
    row = lax.broadcasted_iota(jnp.int32, (t, t), 0)
    col = lax.broadcasted_iota(jnp.int32, (t, t), 1)
    suffix = jnp.where(row > col, 1.0, 0.0).astype(BF16)
    causal = col < row

    def block(j, diagonal):
        start = pl.multiple_of(j * t, t)
        worst = None
        for hh in range(HEADS_PER_STEP):
            lanes = slice(hh * SB_HEAD_DIM, (hh + 1) * SB_HEAD_DIM)
            kj = k_ref[pl.ds(start, t), lanes]
            vj = v_ref[pl.ds(start, t), lanes]
            z = lax.dot_general(q_ref[:, lanes], kj, (((1,), (1,)), ((), ())),
                                preferred_element_type=F32)
            log_beta = jnp.minimum(z, 0.0) - jnp.log(1.0 + jnp.exp(-jnp.abs(z)))
            lom = log_beta - z
            if diagonal:
                lom = jnp.where(causal, lom, 0.0)
            hi = lom.astype(BF16)
            lo = (lom - hi.astype(F32)).astype(BF16)
            after = _dot(hi, suffix) + _dot(lo, suffix)
            carry = carry_ref[hh]
            w = jnp.exp(log_beta + after + carry)
            if diagonal:
                w = jnp.where(causal, w, 0.0)
            acc_ref[hh] += _dot(w.astype(BF16), vj)
            carry = carry + after[:, 0:1] + lom[:, 0:1]
            carry_ref[hh] = carry
            top = jnp.max(carry)
            worst = top if worst is None else jnp.maximum(worst, top)
        return worst

    acc_ref[...] = jnp.zeros_like(acc_ref)
    carry_ref[...] = jnp.zeros_like(carry_ref)
    worst = block(i, True)

    def more(state):
        j, worst = state
        return jnp.logical_and(j >= 0, worst > SB_EXIT_LOG)

    def step(state):
        j, _ = state
        return j - 1, block(j, False)

    lax.while_loop(more, step, (i - 1, worst))
    for hh in range(HEADS_PER_STEP):
        o_ref[:, hh * SB_HEAD_DIM:(hh + 1) * SB_HEAD_DIM] = acc_ref[hh].astype(BF16)


def _attention(qkv, batch, seq):
    n = qkv.shape[0]
    t = ATT_TILE
    q_blocks = seq // t
    pairs = SB_HEADS // HEADS_PER_STEP
    return pl.pallas_call(
        _attn_kernel,
        grid=(batch, pairs, q_blocks),
        in_specs=[
            pl.BlockSpec((t, LANES), lambda b, p, i: (b * q_blocks + i, p)),
            pl.BlockSpec((seq, LANES), lambda b, p, i: (b, pairs + p)),
            pl.BlockSpec((seq, LANES), lambda b, p, i: (b, 2 * pairs + p)),
        ],
        out_specs=pl.BlockSpec((t, LANES), lambda b, p, i: (b * q_blocks + i, p)),
        out_shape=jax.ShapeDtypeStruct((n, SB_WIDTH), BF16),
        scratch_shapes=[pltpu.VMEM((HEADS_PER_STEP, t, SB_HEAD_DIM), F32),
                        pltpu.VMEM((HEADS_PER_STEP, t, 1), F32)],
        compiler_params=_params("arbitrary", "arbitrary", "arbitrary"),
        name="sb_attention",
    )(qkv, qkv, qkv)


def _gelu_tanh(x):
    c = math.sqrt(2.0 / math.pi)
    return 0.5 * x * (1.0 + jnp.tanh(c * (x + 0.044715 * (x * x * x))))


def _s5_kernel(u_ref, wb_ref, einv_re_ref, einv_im_ref, e_re_ref, e_im_ref,
               wc_ref, d_ref, wglu_ref, bglu_ref, o_ref, st_re_ref, st_im_ref):
    ell = SSM_CHUNK

    @pl.when(pl.program_id(1) == 0)
    def _():
        st_re_ref[...] = jnp.zeros_like(st_re_ref)
        st_im_ref[...] = jnp.zeros_like(st_im_ref)

    row = lax.broadcasted_iota(jnp.int32, (ell, ell), 0)
    col = lax.broadcasted_iota(jnp.int32, (ell, ell), 1)
    prefix = jnp.where(col <= row, 1.0, 0.0).astype(BF16)

    def chunk(c, carry):
        rows = pl.ds(pl.multiple_of(c * ell, ell), ell)
        u = u_ref[rows, :]
        bu = _dot(u.astype(BF16), wb_ref[...])
        bu_re = bu[:, :N_STATE]
        bu_im = bu[:, N_STATE:]
        einv_re = einv_re_ref[...]
        einv_im = einv_im_ref[...]
        x_re = bu_re * einv_re - bu_im * einv_im
        x_im = bu_re * einv_im + bu_im * einv_re
        a_re = _dot(prefix, x_re.astype(BF16)) + st_re_ref[...]
        a_im = _dot(prefix, x_im.astype(BF16)) + st_im_ref[...]
        e_re = e_re_ref[...]
        e_im = e_im_ref[...]
        h_re = e_re * a_re - e_im * a_im
        h_im = e_re * a_im + e_im * a_re
        st_re_ref[...] = h_re[ell - 1:ell, :]
        st_im_ref[...] = h_im[ell - 1:ell, :]
        y = (_dot(h_re.astype(BF16), wc_ref[:N_STATE, :])
             + _dot(h_im.astype(BF16), wc_ref[N_STATE:, :]))
        y = _gelu_tanh(y + d_ref[...] * u)
        gate = _dot(y.astype(BF16), wglu_ref[...]) + bglu_ref[...]
        o_ref[rows, :] = (y * _sigmoid(gate)).astype(BF16)
        return carry

    lax.fori_loop(0, SSM_ROWS // ell, chunk, 0)


def _s5_tables(a_re, a_im, log_dt, b_re, b_im, c_re, c_im):
    g, p, cg = SSM_GROUPS, SSM_STATE, SSM_GROUP
    dt = jnp.exp(log_dt)[:, None]
    ld_re, ld_im = a_re * dt, a_im * dt

    def cexp(re, im):
        mag = jnp.exp(re)
        return mag * jnp.cos(im), mag * jnp.sin(im)

    lb_re, lb_im = cexp(ld_re, ld_im)
    den = a_re * a_re + a_im * a_im
    f_re = ((lb_re - 1.0) * a_re + lb_im * a_im) / den
    f_im = (lb_im * a_re - (lb_re - 1.0) * a_im) / den
    bb_re = f_re[..., None] * b_re - f_im[..., None] * b_im
    bb_im = f_re[..., None] * b_im + f_im[..., None] * b_re
    eye = jnp.eye(g, dtype=F32)
    wb_re = jnp.einsum('gpc,gh->gchp', bb_re, eye).reshape(g * cg, g * p)
    wb_im = jnp.einsum('gpc,gh->gchp', bb_im, eye).reshape(g * cg, g * p)
    wb = jnp.concatenate([wb_re, wb_im], axis=1).astype(BF16)
    wc_re = jnp.einsum('gcp,gh->gphc', c_re, eye).reshape(g * p, g * cg)
    wc_im = jnp.einsum('gcp,gh->gphc', c_im, eye).reshape(g * p, g * cg)
    wc = jnp.concatenate([wc_re, -wc_im], axis=0).astype(BF16)
    steps = jnp.arange(1, SSM_CHUNK + 1, dtype=F32)[:, None, None]
    e_re, e_im = cexp(ld_re[None] * steps, ld_im[None] * steps)
    v_re, v_im = cexp(-ld_re[None] * steps, -ld_im[None] * steps)
    flat = lambda a: a.reshape(SSM_CHUNK, g * p)
    return wb, flat(v_re), flat(v_im), flat(e_re), flat(e_im), wc


def _s5(u, tables, d_skip, w_glu_bf16, b_glu, batch, seq):
    n = u.shape[0]
    wb, einv_re, einv_im, e_re, e_im, wc = tables
    steps = seq // SSM_ROWS
    tab = (SSM_CHUNK, N_STATE)
    return pl.pallas_call(
        _s5_kernel,
        grid=(batch, steps),
        in_specs=[
            pl.BlockSpec((SSM_ROWS, SSM_WIDTH), lambda b, i: (b * steps + i, 0)),
            _resident((SSM_WIDTH, 2 * N_STATE)),
            _resident(tab), _resident(tab), _resident(tab), _resident(tab),
            _resident((2 * N_STATE, SSM_WIDTH)),
            _resident((1, SSM_WIDTH)),
            _resident((SSM_WIDTH, SSM_WIDTH)),
            _resident((1, SSM_WIDTH)),
        ],
        out_specs=pl.BlockSpec((SSM_ROWS, SSM_WIDTH), lambda b, i: (b * steps + i, 0)),
        out_shape=jax.ShapeDtypeStruct((n, SSM_WIDTH), BF16),
        scratch_shapes=[pltpu.VMEM((1, N_STATE), F32), pltpu.VMEM((1, N_STATE), F32)],
        compiler_params=_params("arbitrary", "arbitrary"),
        name="s5_scan",
    )(u, wb, einv_re, einv_im, e_re, e_im, wc,
      d_skip.reshape(1, SSM_WIDTH), w_glu_bf16, b_glu.reshape(1, SSM_WIDTH))


def _mixout_kernel(x_ref, att_ref, s5_ref, g_ref, mod_ref, wsb_ref, wssm_ref,
                   wout_ref, lng_ref, lnb_ref, o_ref, *, alpha):
    y_sb = _dot(att_ref[...], wsb_ref[...])
    y_ssm = _dot(s5_ref[...], wssm_ref[...])
    g_sb = g_ref[:, :D_MODEL].astype(F32)
    g_ssm = g_ref[:, D_MODEL:].astype(F32)
    merged = _sigmoid(g_sb) * y_sb + _sigmoid(g_ssm) * y_ssm
    y = _dot(merged.astype(BF16), wout_ref[...])
    gate = mod_ref[2:3, :]
    r = alpha * x_ref[...] + (1.0 + gate) * y
    o_ref[...] = _normalize(r) * lng_ref[...] + lnb_ref[...]


def _mix_out(x2d, att, s5, g, mod, w_sb_up, w_ssm_up, w_out, ln_g, ln_b, seq, alpha):
    n = x2d.shape[0]
    tiles_per_seq = seq // ROW_TILE
    rows = lambda width: pl.BlockSpec((ROW_TILE, width), lambda i: (i, 0))
    return pl.pallas_call(
        functools.partial(_mixout_kernel, alpha=alpha),
        grid=(n // ROW_TILE,),
        in_specs=[
            rows(D_MODEL), rows(SB_WIDTH), rows(SSM_WIDTH), rows(2 * D_MODEL),
            pl.BlockSpec((None, N_MOD, D_MODEL), lambda i: (i // tiles_per_seq, 0, 0)),
            _resident((SB_WIDTH, D_MODEL)),
            _resident((SSM_WIDTH, D_MODEL)),
            _resident((D_MODEL, D_MODEL)),
            _resident((1, D_MODEL)),
            _resident((1, D_MODEL)),
        ],
        out_specs=rows(D_MODEL),
        out_shape=jax.ShapeDtypeStruct((n, D_MODEL), F32),
        compiler_params=_params("arbitrary"),
        name="mix_out",
    )(x2d, att, s5, g, mod, w_sb_up, w_ssm_up, w_out,
      ln_g.reshape(1, D_MODEL), ln_b.reshape(1, D_MODEL))


def _ffn_kernel(x_ref, mod_ref, win_ref, wout_ref, lng_ref, lnb_ref, o_ref,
                h_ref, acc_ref, *, alpha):
    x = x_ref[...]
    shift = mod_ref[3:4, :]
    scale = mod_ref[4:5, :]
    gate_mod = mod_ref[5:6, :]
    h_ref[...] = (_normalize(x) * (1.0 + scale) + shift).astype(BF16)
    h = h_ref[...]
    for j in range(FFN_HIDDEN // FFN_CHUNK):
        cols = slice(j * FFN_CHUNK, (j + 1) * FFN_CHUNK)
        up_cols = slice(FFN_HIDDEN + j * FFN_CHUNK, FFN_HIDDEN + (j + 1) * FFN_CHUNK)
        gate = _dot(h, win_ref[:, cols])
        up = _dot(h, win_ref[:, up_cols])
        act = (gate * _sigmoid(gate) * up).astype(BF16)
        part = _dot(act, wout_ref[cols, :])
        if j == 0:
            acc_ref[...] = part
        else:
            acc_ref[...] += part
    r = alpha * x + (1.0 + gate_mod) * acc_ref[...]
    o_ref[...] = _normalize(r) * lng_ref[...] + lnb_ref[...]


def _ffn(x2d, mod, w_ffn_in, w_ffn_out, ln_g, ln_b, seq, alpha):
    n = x2d.shape[0]
    tiles_per_seq = seq // ROW_TILE
    return pl.pallas_call(
        functools.partial(_ffn_kernel, alpha=alpha),
        grid=(n // ROW_TILE,),
        in_specs=[
            pl.BlockSpec((ROW_TILE, D_MODEL), lambda i: (i, 0)),
            pl.BlockSpec((None, N_MOD, D_MODEL), lambda i: (i // tiles_per_seq, 0, 0)),
            _resident((D_MODEL, 2 * FFN_HIDDEN)),
            _resident((FFN_HIDDEN, D_MODEL)),
            _resident((1, D_MODEL)),
            _resident((1, D_MODEL)),
        ],
        out_specs=pl.BlockSpec((ROW_TILE, D_MODEL), lambda i: (i, 0)),
        out_shape=jax.ShapeDtypeStruct((n, D_MODEL), F32),
        scratch_shapes=[pltpu.VMEM((ROW_TILE, D_MODEL), BF16),
                        pltpu.VMEM((ROW_TILE, D_MODEL), F32)],
        compiler_params=_params("arbitrary"),
        name="ffn",
    )(x2d, mod, w_ffn_in, w_ffn_out, ln_g.reshape(1, D_MODEL), ln_b.reshape(1, D_MODEL))


def kernel(x, c, w_ada, b_ada, w_in, w_sb_up, ssm_a_re, ssm_a_im, ssm_log_dt,
           ssm_b_re, ssm_b_im, ssm_c_re, ssm_c_im, ssm_d, w_glu, b_glu,
           w_ssm_up, w_out, ln1_g, ln1_b, w_ffn_in, w_ffn_out, ln2_g, ln2_b):
    batch, seq, d = x.shape
    depth = w_in.shape[0]
    assert d == D_MODEL and seq % max(ROW_TILE, ATT_TILE, SSM_ROWS) == 0
    alpha = (2 * depth) ** 0.25
    mod = _modulation(c, w_ada, b_ada)
    x2d = x.reshape(batch * seq, d)
    for l in range(depth):
        qkv, u, g = _in_projection(x2d, mod[l], w_in[l].astype(BF16), seq)
        att = _attention(qkv, batch, seq)
        tables = _s5_tables(ssm_a_re[l], ssm_a_im[l], ssm_log_dt[l], ssm_b_re[l],
                            ssm_b_im[l], ssm_c_re[l], ssm_c_im[l])
        s5 = _s5(u, tables, ssm_d[l], w_glu[l].astype(BF16), b_glu[l], batch, seq)
        x2d = _mix_out(x2d, att, s5, g, mod[l], w_sb_up[l].astype(BF16),
                       w_ssm_up[l].astype(BF16), w_out[l].astype(BF16),
                       ln1_g[l], ln1_b[l], seq, alpha)
        x2d = _ffn(x2d, mod[l], w_ffn_in[l].astype(BF16), w_ffn_out[l].astype(BF16),
                   ln2_g[l], ln2_b[l], seq, alpha)
    return x2d.reshape(batch, seq, d)
```

```python
import functools
import math

import jax
import jax.numpy as jnp
from jax import lax
from jax.experimental import pallas as pl
from jax.experimental.pallas import tpu as pltpu

F32 = jnp.float32
BF16 = jnp.bfloat16

D_MODEL = 1024
SB_HEADS = 8
SB_HEAD_DIM = 64
SB_WIDTH = SB_HEADS * SB_HEAD_DIM
SSM_WIDTH = D_MODEL // 2
SSM_GROUP = 16
SSM_GROUPS = SSM_WIDTH // SSM_GROUP
SSM_STATE = 64
N_STATE = SSM_GROUPS * SSM_STATE
FFN_HIDDEN = 2816
IN_COLS = 3 * SB_WIDTH + SSM_WIDTH + 2 * D_MODEL
N_MOD = 6
LN_EPS = 1e-5

LANES = 128
SUBLANES = 8
VMEM_LIMIT_BYTES = 56 * 1024 * 1024

ROW_TILE = 512
ATT_TILE = 256
SSM_CHUNK = 64
SSM_ROWS = 512
SSM_BLOCK = 512
SSM_BLOCK_CH = SSM_BLOCK // SSM_STATE * SSM_GROUP
SCAN_ROWS = 256
FFN_CHUNK = 256
HEADS_PER_STEP = LANES // SB_HEAD_DIM
SB_EXIT_LOG = -105.0


def _sigmoid(x):
    return 1.0 / (1.0 + jnp.exp(-x))


def _normalize(x):
    mu = jnp.mean(x, axis=-1, keepdims=True)
    xc = x - mu
    var = jnp.mean(xc * xc, axis=-1, keepdims=True)
    return xc * lax.rsqrt(var + LN_EPS)


def _dot(a, b):
    return jnp.dot(a, b, preferred_element_type=F32)


def _resident(shape):
    zeros = (0,) * len(shape)
    return pl.BlockSpec(shape, lambda *_: zeros, pipeline_mode=pl.Buffered(1))


def _params(*semantics):
    return pltpu.CompilerParams(dimension_semantics=semantics,
                                vmem_limit_bytes=VMEM_LIMIT_BYTES)


def _mod_kernel(c_ref, w_ref, b_ref, o_ref):
    c = c_ref[...]
    c_act = c * _sigmoid(c)
    o_ref[...] = jnp.dot(c_act, w_ref[...], preferred_element_type=F32,
                         precision=lax.Precision.HIGHEST) + b_ref[...]


def _modulation(c, w_ada, b_ada):
    depth = w_ada.shape[0]
    batch = c.shape[0]
    rows = -(-batch // SUBLANES) * SUBLANES
    c_pad = jnp.pad(c, ((0, rows - batch), (0, 0)))
    cols = N_MOD * D_MODEL
    out = pl.pallas_call(
        _mod_kernel,
        grid=(depth, N_MOD),
        in_specs=[
            pl.BlockSpec((rows, D_MODEL), lambda l, j: (0, 0)),
            pl.BlockSpec((None, D_MODEL, D_MODEL), lambda l, j: (l, 0, j)),
            pl.BlockSpec((None, 1, D_MODEL), lambda l, j: (l, 0, j)),
        ],
        out_specs=pl.BlockSpec((None, rows, D_MODEL), lambda l, j: (l, 0, j)),
        out_shape=jax.ShapeDtypeStruct((depth, rows, cols), F32),
        compiler_params=_params("arbitrary", "arbitrary"),
        name="adaln_mod",
    )(c_pad, w_ada, b_ada.reshape(depth, 1, cols))
    return out[:, :batch].reshape(depth, batch, N_MOD, D_MODEL)


def _inproj_kernel(x_ref, mod_ref, w_ref, qkv_ref, u_ref, g_ref, h_ref):
    shift = mod_ref[0:1, :]
    scale = mod_ref[1:2, :]
    h_ref[...] = (_normalize(x_ref[...]) * (1.0 + scale) + shift).astype(BF16)
    h = h_ref[...]
    w = SB_WIDTH
    qkv_ref[:, 0:w] = (_dot(h, w_ref[:, 0:w]) * (1.0 / math.sqrt(SB_HEAD_DIM))).astype(BF16)
    qkv_ref[:, w:2 * w] = _dot(h, w_ref[:, w:2 * w]).astype(BF16)
    qkv_ref[:, 2 * w:3 * w] = _dot(h, w_ref[:, 2 * w:3 * w]).astype(BF16)
    u_ref[...] = _dot(h, w_ref[:, 3 * w:3 * w + SSM_WIDTH])
    g0 = 3 * w + SSM_WIDTH
    for j in range(2 * D_MODEL // w):
        g_ref[:, j * w:(j + 1) * w] = _dot(h, w_ref[:, g0 + j * w:g0 + (j + 1) * w]).astype(BF16)


def _in_projection(x2d, mod, w_in_bf16, seq):
    n = x2d.shape[0]
    tiles_per_seq = seq // ROW_TILE
    return pl.pallas_call(
        _inproj_kernel,
        grid=(n // ROW_TILE,),
        in_specs=[
            pl.BlockSpec((ROW_TILE, D_MODEL), lambda i: (i, 0)),
            pl.BlockSpec((None, N_MOD, D_MODEL), lambda i: (i // tiles_per_seq, 0, 0)),
            _resident((D_MODEL, IN_COLS)),
        ],
        out_specs=[
            pl.BlockSpec((ROW_TILE, 3 * SB_WIDTH), lambda i: (i, 0)),
            pl.BlockSpec((ROW_TILE, SSM_WIDTH), lambda i: (i, 0)),
            pl.BlockSpec((ROW_TILE, 2 * D_MODEL), lambda i: (i, 0)),
        ],
        out_shape=[
            jax.ShapeDtypeStruct((n, 3 * SB_WIDTH), BF16),
            jax.ShapeDtypeStruct((n, SSM_WIDTH), F32),
            jax.ShapeDtypeStruct((n, 2 * D_MODEL), BF16),
        ],
        scratch_shapes=[pltpu.VMEM((ROW_TILE, D_MODEL), BF16)],
        compiler_params=_params("arbitrary"),
        name="in_proj",
    )(x2d, mod, w_in_bf16)


def _attn_kernel(q_ref, k_ref, v_ref, o_ref, acc_ref, carry_ref):
    t = ATT_TILE
    i = pl.program_id(2)
    row = lax.broadcasted_iota(jnp.int32, (t, t), 0)
    col = lax.broadcasted_iota(jnp.int32, (t, t), 1)
    suffix = jnp.where(row > col, 1.0, 0.0).astype(BF16)
    causal = col < row

    def visit(blocks):
        chains = []
        for j, diagonal in blocks:
            start = pl.multiple_of(j * t, t)
            for hh in range(HEADS_PER_STEP):
                lanes = slice(hh * SB_HEAD_DIM, (hh + 1) * SB_HEAD_DIM)
                kj = k_ref[pl.ds(start, t), lanes]
                z = lax.dot_general(q_ref[:, lanes], kj, (((1,), (1,)), ((), ())),
                                    preferred_element_type=F32)
                chains.append(dict(hh=hh, diagonal=diagonal, start=start, lanes=lanes, z=z))
        for c in chains:
            z = c.pop("z")
            log_beta = jnp.minimum(z, 0.0) - jnp.log(1.0 + jnp.exp(-jnp.abs(z)))
            lom = log_beta - z
            if c["diagonal"]:
                lom = jnp.where(causal, lom, 0.0)
            hi = lom.astype(BF16)
            lo = (lom - hi.astype(F32)).astype(BF16)
            c.update(log_beta=log_beta, first=lom[:, 0:1], hi=hi, lo=lo)
        for c in chains:
            c["after"] = _dot(c.pop("hi"), suffix) + _dot(c.pop("lo"), suffix)
        carries = [carry_ref[hh] for hh in range(HEADS_PER_STEP)]
        for c in chains:
            hh, after = c["hh"], c["after"]
            w = jnp.exp(c["log_beta"] + after + carries[hh])
            if c["diagonal"]:
                w = jnp.where(causal, w, 0.0)
            vj = v_ref[pl.ds(c["start"], t), c["lanes"]]
            acc_ref[hh] += _dot(w.astype(BF16), vj)
            carries[hh] = carries[hh] + after[:, 0:1] + c["first"]
        worst = None
        for hh in range(HEADS_PER_STEP):
            carry_ref[hh] = carries[hh]
            top = jnp.max(carries[hh])
            worst = top if worst is None else jnp.maximum(worst, top)
        return worst

    acc_ref[...] = jnp.zeros_like(acc_ref)
    carry_ref[...] = jnp.zeros_like(carry_ref)

    worst = lax.cond(i > 0,
                     lambda: visit([(i, True), (i - 1, False)]),
                     lambda: visit([(i, True)]))

    def more(state):
        j, worst = state
        return jnp.logical_and(j >= 0, worst > SB_EXIT_LOG)

    def step(state):
        j, _ = state
        return j - 1, visit([(j, False)])

    lax.while_loop(more, step, (i - 2, worst))
    for hh in range(HEADS_PER_STEP):
        o_ref[:, hh * SB_HEAD_DIM:(hh + 1) * SB_HEAD_DIM] = acc_ref[hh].astype(BF16)


def _attention(qkv, batch, seq):
    n = qkv.shape[0]
    t = ATT_TILE
    q_blocks = seq // t
    pairs = SB_HEADS // HEADS_PER_STEP
    return pl.pallas_call(
        _attn_kernel,
        grid=(batch, pairs, q_blocks),
        in_specs=[
            pl.BlockSpec((t, LANES), lambda b, p, i: (b * q_blocks + i, p)),
            pl.BlockSpec((seq, LANES), lambda b, p, i: (b, pairs + p)),
            pl.BlockSpec((seq, LANES), lambda b, p, i: (b, 2 * pairs + p)),
        ],
        out_specs=pl.BlockSpec((t, LANES), lambda b, p, i: (b * q_blocks + i, p)),
        out_shape=jax.ShapeDtypeStruct((n, SB_WIDTH), BF16),
        scratch_shapes=[pltpu.VMEM((HEADS_PER_STEP, t, SB_HEAD_DIM), F32),
                        pltpu.VMEM((HEADS_PER_STEP, t, 1), F32)],
        compiler_params=_params("arbitrary", "arbitrary", "arbitrary"),
        name="sb_attention",
    )(qkv, qkv, qkv)


def _gelu_tanh(x):
    c = math.sqrt(2.0 / math.pi)
    return 0.5 * x * (1.0 + jnp.tanh(c * (x + 0.044715 * (x * x * x))))


def _s5_kernel(u_ref, wb_ref, einv_re_ref, einv_im_ref, e_re_ref, e_im_ref,
               wc_ref, d_ref, wglu_ref, bglu_ref, o_ref,
               st_ref, bu_ref, xs_ref, cum_ref, hs_ref, y_ref):
    ell = SSM_CHUNK
    nb = SSM_BLOCK
    n_chunks = SSM_ROWS // ell

    @pl.when(pl.program_id(1) == 0)
    def _():
        st_ref[...] = jnp.zeros_like(st_ref)

    row = lax.broadcasted_iota(jnp.int32, (SCAN_ROWS, SCAN_ROWS), 0)
    col = lax.broadcasted_iota(jnp.int32, (SCAN_ROWS, SCAN_ROWS), 1)
    same_chunk = (row // ell) == (col // ell)
    prefix = jnp.where(jnp.logical_and(same_chunk, col <= row), 1.0, 0.0).astype(BF16)

    u = u_ref[...]
    u16 = u.astype(BF16)
    for sb in range(N_STATE // nb):
        states = slice(sb * nb, (sb + 1) * nb)
        chans = slice(sb * SSM_BLOCK_CH, (sb + 1) * SSM_BLOCK_CH)
        bu_ref[...] = _dot(u16[:, chans], wb_ref[sb])
        v_re = einv_re_ref[:, states]
        v_im = einv_im_ref[:, states]
        for c in range(n_chunks):
            rows = slice(c * ell, (c + 1) * ell)
            b_re = bu_ref[rows, :nb]
            b_im = bu_ref[rows, nb:]
            xs_ref[rows, :nb] = (b_re * v_re - b_im * v_im).astype(BF16)
            xs_ref[rows, nb:] = (b_re * v_im + b_im * v_re).astype(BF16)
        for r0 in range(0, SSM_ROWS, SCAN_ROWS):
            slab = slice(r0, r0 + SCAN_ROWS)
            cum_ref[slab, :] = _dot(prefix, xs_ref[slab, :])
        e_re = e_re_ref[:, states]
        e_im = e_im_ref[:, states]
        st_re = st_ref[0:1, states]
        st_im = st_ref[1:2, states]
        for c in range(n_chunks):
            rows = slice(c * ell, (c + 1) * ell)
            a_re = cum_ref[rows, :nb] + st_re
            a_im = cum_ref[rows, nb:] + st_im
            h_re = e_re * a_re - e_im * a_im
            h_im = e_re * a_im + e_im * a_re
            st_re = h_re[ell - 1:ell, :]
            st_im = h_im[ell - 1:ell, :]
            hs_ref[rows, :nb] = h_re.astype(BF16)
            hs_ref[rows, nb:] = h_im.astype(BF16)
        st_ref[0:1, states] = st_re
        st_ref[1:2, states] = st_im
        y_ref[:, chans] = _dot(hs_ref[...], wc_ref[sb])
    y = _gelu_tanh(y_ref[...] + d_ref[...] * u)
    gate = _dot(y.astype(BF16), wglu_ref[...]) + bglu_ref[...]
    o_ref[...] = (y * _sigmoid(gate)).astype(BF16)


def _s5_tables(a_re, a_im, log_dt, b_re, b_im, c_re, c_im):
    g, p, cg = SSM_GROUPS, SSM_STATE, SSM_GROUP
    gb = SSM_BLOCK // p
    nblk = g // gb
    dt = jnp.exp(log_dt)[:, None]
    ld_re, ld_im = a_re * dt, a_im * dt

    def cexp(re, im):
        mag = jnp.exp(re)
        return mag * jnp.cos(im), mag * jnp.sin(im)

    lb_re, lb_im = cexp(ld_re, ld_im)
    den = a_re * a_re + a_im * a_im
    f_re = ((lb_re - 1.0) * a_re + lb_im * a_im) / den
    f_im = (lb_im * a_re - (lb_re - 1.0) * a_im) / den
    bb_re = f_re[..., None] * b_re - f_im[..., None] * b_im
    bb_im = f_re[..., None] * b_im + f_im[..., None] * b_re
    eye = jnp.eye(gb, dtype=F32)

    def in_map(b):
        blocks = jnp.einsum('bgpc,gh->bgchp', b.reshape(nblk, gb, p, cg), eye)
        return blocks.reshape(nblk, gb * cg, gb * p)

    def out_map(c):
        blocks = jnp.einsum('bgcp,gh->bgphc', c.reshape(nblk, gb, cg, p), eye)
        return blocks.reshape(nblk, gb * p, gb * cg)

    wb = jnp.concatenate([in_map(bb_re), in_map(bb_im)], axis=2).astype(BF16)
    wc = jnp.concatenate([out_map(c_re), -out_map(c_im)], axis=1).astype(BF16)
    steps = jnp.arange(1, SSM_CHUNK + 1, dtype=F32)[:, None, None]
    e_re, e_im = cexp(ld_re[None] * steps, ld_im[None] * steps)
    v_re, v_im = cexp(-ld_re[None] * steps, -ld_im[None] * steps)
    flat = lambda a: a.reshape(SSM_CHUNK, g * p)
    return wb, flat(v_re), flat(v_im), flat(e_re), flat(e_im), wc


def _s5(u, tables, d_skip, w_glu_bf16, b_glu, batch, seq):
    n = u.shape[0]
    wb, einv_re, einv_im, e_re, e_im, wc = tables
    steps = seq // SSM_ROWS
    tab = (SSM_CHUNK, N_STATE)
    return pl.pallas_call(
        _s5_kernel,
        grid=(batch, steps),
        in_specs=[
            pl.BlockSpec((SSM_ROWS, SSM_WIDTH), lambda b, i: (b * steps + i, 0)),
            _resident(wb.shape),
            _resident(tab), _resident(tab), _resident(tab), _resident(tab),
            _resident(wc.shape),
            _resident((1, SSM_WIDTH)),
            _resident((SSM_WIDTH, SSM_WIDTH)),
            _resident((1, SSM_WIDTH)),
        ],
        out_specs=pl.BlockSpec((SSM_ROWS, SSM_WIDTH), lambda b, i: (b * steps + i, 0)),
        out_shape=jax.ShapeDtypeStruct((n, SSM_WIDTH), BF16),
        scratch_shapes=[
            pltpu.VMEM((2, N_STATE), F32),
            pltpu.VMEM((SSM_ROWS, 2 * SSM_BLOCK), F32),
            pltpu.VMEM((SSM_ROWS, 2 * SSM_BLOCK), BF16),
            pltpu.VMEM((SSM_ROWS, 2 * SSM_BLOCK), F32),
            pltpu.VMEM((SSM_ROWS, 2 * SSM_BLOCK), BF16),
            pltpu.VMEM((SSM_ROWS, SSM_WIDTH), F32),
        ],
        compiler_params=_params("arbitrary", "arbitrary"),
        name="s5_scan",
    )(u, wb, einv_re, einv_im, e_re, e_im, wc,
      d_skip.reshape(1, SSM_WIDTH), w_glu_bf16, b_glu.reshape(1, SSM_WIDTH))


def _mixout_kernel(x_ref, att_ref, s5_ref, g_ref, mod_ref, wsb_ref, wssm_ref,
                   wout_ref, lng_ref, lnb_ref, o_ref, *, alpha):
    y_sb = _dot(att_ref[...], wsb_ref[...])
    y_ssm = _dot(s5_ref[...], wssm_ref[...])
    g_sb = g_ref[:, :D_MODEL].astype(F32)
    g_ssm = g_ref[:, D_MODEL:].astype(F32)
    merged = _sigmoid(g_sb) * y_sb + _sigmoid(g_ssm) * y_ssm
    y = _dot(merged.astype(BF16), wout_ref[...])
    gate = mod_ref[2:3, :]
    r = alpha * x_ref[...] + (1.0 + gate) * y
    o_ref[...] = _normalize(r) * lng_ref[...] + lnb_ref[...]


def _mix_out(x2d, att, s5, g, mod, w_sb_up, w_ssm_up, w_out, ln_g, ln_b, seq, alpha):
    n = x2d.shape[0]
    tiles_per_seq = seq // ROW_TILE
    rows = lambda width: pl.BlockSpec((ROW_TILE, width), lambda i: (i, 0))
    return pl.pallas_call(
        functools.partial(_mixout_kernel, alpha=alpha),
        grid=(n // ROW_TILE,),
        in_specs=[
            rows(D_MODEL), rows(SB_WIDTH), rows(SSM_WIDTH), rows(2 * D_MODEL),
            pl.BlockSpec((None, N_MOD, D_MODEL), lambda i: (i // tiles_per_seq, 0, 0)),
            _resident((SB_WIDTH, D_MODEL)),
            _resident((SSM_WIDTH, D_MODEL)),
            _resident((D_MODEL, D_MODEL)),
            _resident((1, D_MODEL)),
            _resident((1, D_MODEL)),
        ],
        out_specs=rows(D_MODEL),
        out_shape=jax.ShapeDtypeStruct((n, D_MODEL), F32),
        compiler_params=_params("arbitrary"),
        name="mix_out",
    )(x2d, att, s5, g, mod, w_sb_up, w_ssm_up, w_out,
      ln_g.reshape(1, D_MODEL), ln_b.reshape(1, D_MODEL))


def _ffn_kernel(x_ref, mod_ref, win_ref, wout_ref, lng_ref, lnb_ref, o_ref,
                h_ref, acc_ref, *, alpha):
    x = x_ref[...]
    shift = mod_ref[3:4, :]
    scale = mod_ref[4:5, :]
    gate_mod = mod_ref[5:6, :]
    h_ref[...] = (_normalize(x) * (1.0 + scale) + shift).astype(BF16)
    h = h_ref[...]
    for j in range(FFN_HIDDEN // FFN_CHUNK):
        cols = slice(j * FFN_CHUNK, (j + 1) * FFN_CHUNK)
        up_cols = slice(FFN_HIDDEN + j * FFN_CHUNK, FFN_HIDDEN + (j + 1) * FFN_CHUNK)
        gate = _dot(h, win_ref[:, cols])
        up = _dot(h, win_ref[:, up_cols])
        act = (gate * _sigmoid(gate) * up).astype(BF16)
        part = _dot(act, wout_ref[cols, :])
        if j == 0:
            acc_ref[...] = part
        else:
            acc_ref[...] += part
    r = alpha * x + (1.0 + gate_mod) * acc_ref[...]
    o_ref[...] = _normalize(r) * lng_ref[...] + lnb_ref[...]


def _ffn(x2d, mod, w_ffn_in, w_ffn_out, ln_g, ln_b, seq, alpha):
    n = x2d.shape[0]
    tiles_per_seq = seq // ROW_TILE
    return pl.pallas_call(
        functools.partial(_ffn_kernel, alpha=alpha),
        grid=(n // ROW_TILE,),
        in_specs=[
            pl.BlockSpec((ROW_TILE, D_MODEL), lambda i: (i, 0)),
            pl.BlockSpec((None, N_MOD, D_MODEL), lambda i: (i // tiles_per_seq, 0, 0)),
            _resident((D_MODEL, 2 * FFN_HIDDEN)),
            _resident((FFN_HIDDEN, D_MODEL)),
            _resident((1, D_MODEL)),
            _resident((1, D_MODEL)),
        ],
        out_specs=pl.BlockSpec((ROW_TILE, D_MODEL), lambda i: (i, 0)),
        out_shape=jax.ShapeDtypeStruct((n, D_MODEL), F32),
        scratch_shapes=[pltpu.VMEM((ROW_TILE, D_MODEL), BF16),
                        pltpu.VMEM((ROW_TILE, D_MODEL), F32)],
        compiler_params=_params("arbitrary"),
        name="ffn",
    )(x2d, mod, w_ffn_in, w_ffn_out, ln_g.reshape(1, D_MODEL), ln_b.reshape(1, D_MODEL))


def kernel(x, c, w_ada, b_ada, w_in, w_sb_up, ssm_a_re, ssm_a_im, ssm_log_dt,
           ssm_b_re, ssm_b_im, ssm_c_re, ssm_c_im, ssm_d, w_glu, b_glu,
           w_ssm_up, w_out, ln1_g, ln1_b, w_ffn_in, w_ffn_out, ln2_g, ln2_b):
    batch, seq, d = x.shape
    depth = w_in.shape[0]
    assert d == D_MODEL and seq % max(ROW_TILE, ATT_TILE, SSM_ROWS) == 0
    alpha = (2 * depth) ** 0.25
    mod = _modulation(c, w_ada, b_ada)
    x2d = x.reshape(batch * seq, d)
    for l in range(depth):
        qkv, u, g = _in_projection(x2d, mod[l], w_in[l].astype(BF16), seq)
        att = _attention(qkv, batch, seq)
        tables = _s5_tables(ssm_a_re[l], ssm_a_im[l], ssm_log_dt[l], ssm_b_re[l],
                            ssm_b_im[l], ssm_c_re[l], ssm_c_im[l])
        s5 = _s5(u, tables, ssm_d[l], w_glu[l].astype(BF16), b_glu[l], batch, seq)
        x2d = _mix_out(x2d, att, s5, g, mod[l], w_sb_up[l].astype(BF16),
                       w_ssm_up[l].astype(BF16), w_out[l].astype(BF16),
                       ln1_g[l], ln1_b[l], seq, alpha)
        x2d = _ffn(x2d, mod[l], w_ffn_in[l].astype(BF16), w_ffn_out[l].astype(BF16),
                   ln2_g[l], ln2_b[l], seq, alpha)
    return x2d.reshape(batch, seq, d)
```

```python
import functools
import math

import jax
import jax.numpy as jnp
from jax import lax
from jax.experimental import pallas as pl
from jax.experimental.pallas import tpu as pltpu

F32 = jnp.float32
BF16 = jnp.bfloat16

D_MODEL = 1024
SB_HEADS = 8
SB_HEAD_DIM = 64
SB_WIDTH = SB_HEADS * SB_HEAD_DIM
SSM_WIDTH = D_MODEL // 2
SSM_GROUP = 16
SSM_GROUPS = SSM_WIDTH // SSM_GROUP
SSM_STATE = 64
N_STATE = SSM_GROUPS * SSM_STATE
FFN_HIDDEN = 2816
IN_COLS = 3 * SB_WIDTH + SSM_WIDTH + 2 * D_MODEL
N_MOD = 6
LN_EPS = 1e-5

LANES = 128
SUBLANES = 8
VMEM_LIMIT_BYTES = 56 * 1024 * 1024

ROW_TILE = 512
ATT_TILE = 256
ATT_BLOCKS_PER_STEP = 2
SSM_CHUNK = 64
SSM_ROWS = 512
SSM_BLOCK = 512
SSM_BLOCK_CH = SSM_BLOCK // SSM_STATE * SSM_GROUP
SCAN_ROWS = 256
FFN_CHUNK = 256
HEADS_PER_STEP = LANES // SB_HEAD_DIM
SB_EXIT_LOG = -105.0


def _sigmoid(x):
    return 1.0 / (1.0 + jnp.exp(-x))


def _normalize(x):
    mu = jnp.mean(x, axis=-1, keepdims=True)
    xc = x - mu
    var = jnp.mean(xc * xc, axis=-1, keepdims=True)
    return xc * lax.rsqrt(var + LN_EPS)


def _dot(a, b):
    return jnp.dot(a, b, preferred_element_type=F32)


def _resident(shape):
    zeros = (0,) * len(shape)
    return pl.BlockSpec(shape, lambda *_: zeros, pipeline_mode=pl.Buffered(1))


def _params(*semantics):
    return pltpu.CompilerParams(dimension_semantics=semantics,
                                vmem_limit_bytes=VMEM_LIMIT_BYTES)


def _mod_kernel(c_ref, w_ref, b_ref, o_ref):
    c = c_ref[...]
    c_act = c * _sigmoid(c)
    o_ref[...] = jnp.dot(c_act, w_ref[...], preferred_element_type=F32,
                         precision=lax.Precision.HIGHEST) + b_ref[...]


def _modulation(c, w_ada, b_ada):
    depth = w_ada.shape[0]
    batch = c.shape[0]
    rows = -(-batch // SUBLANES) * SUBLANES
    c_pad = jnp.pad(c, ((0, rows - batch), (0, 0)))
    cols = N_MOD * D_MODEL
    out = pl.pallas_call(
        _mod_kernel,
        grid=(depth, N_MOD),
        in_specs=[
            pl.BlockSpec((rows, D_MODEL), lambda l, j: (0, 0)),
            pl.BlockSpec((None, D_MODEL, D_MODEL), lambda l, j: (l, 0, j)),
            pl.BlockSpec((None, 1, D_MODEL), lambda l, j: (l, 0, j)),
        ],
        out_specs=pl.BlockSpec((None, rows, D_MODEL), lambda l, j: (l, 0, j)),
        out_shape=jax.ShapeDtypeStruct((depth, rows, cols), F32),
        compiler_params=_params("arbitrary", "arbitrary"),
        name="adaln_mod",
    )(c_pad, w_ada, b_ada.reshape(depth, 1, cols))
    return out[:, :batch].reshape(depth, batch, N_MOD, D_MODEL)


def _inproj_kernel(x_ref, mod_ref, w_ref, qkv_ref, u_ref, g_ref, h_ref):
    shift = mod_ref[0:1, :]
    scale = mod_ref[1:2, :]
    h_ref[...] = (_normalize(x_ref[...]) * (1.0 + scale) + shift).astype(BF16)
    h = h_ref[...]
    w = SB_WIDTH
    qkv_ref[:, 0:w] = (_dot(h, w_ref[:, 0:w]) * (1.0 / math.sqrt(SB_HEAD_DIM))).astype(BF16)
    qkv_ref[:, w:2 * w] = _dot(h, w_ref[:, w:2 * w]).astype(BF16)
    qkv_ref[:, 2 * w:3 * w] = _dot(h, w_ref[:, 2 * w:3 * w]).astype(BF16)
    u_ref[...] = _dot(h, w_ref[:, 3 * w:3 * w + SSM_WIDTH])
    g0 = 3 * w + SSM_WIDTH
    for j in range(2 * D_MODEL // w):
        g_ref[:, j * w:(j + 1) * w] = _dot(h, w_ref[:, g0 + j * w:g0 + (j + 1) * w]).astype(BF16)


def _in_projection(x2d, mod, w_in_bf16, seq):
    n = x2d.shape[0]
    tiles_per_seq = seq // ROW_TILE
    return pl.pallas_call(
        _inproj_kernel,
        grid=(n // ROW_TILE,),
        in_specs=[
            pl.BlockSpec((ROW_TILE, D_MODEL), lambda i: (i, 0)),
            pl.BlockSpec((None, N_MOD, D_MODEL), lambda i: (i // tiles_per_seq, 0, 0)),
            _resident((D_MODEL, IN_COLS)),
        ],
        out_specs=[
            pl.BlockSpec((ROW_TILE, 3 * SB_WIDTH), lambda i: (i, 0)),
            pl.BlockSpec((ROW_TILE, SSM_WIDTH), lambda i: (i, 0)),
            pl.BlockSpec((ROW_TILE, 2 * D_MODEL), lambda i: (i, 0)),
        ],
        out_shape=[
            jax.ShapeDtypeStruct((n, 3 * SB_WIDTH), BF16),
            jax.ShapeDtypeStruct((n, SSM_WIDTH), F32),
            jax.ShapeDtypeStruct((n, 2 * D_MODEL), BF16),
        ],
        scratch_shapes=[pltpu.VMEM((ROW_TILE, D_MODEL), BF16)],
        compiler_params=_params("arbitrary"),
        name="in_proj",
    )(x2d, mod, w_in_bf16)


def _attn_kernel(q_ref, k_ref, v_ref, o_ref, acc_ref, carry_ref):
    t = ATT_TILE
    first_block = pl.program_id(2) * ATT_BLOCKS_PER_STEP
    row = lax.broadcasted_iota(jnp.int32, (t, t), 0)
    col = lax.broadcasted_iota(jnp.int32, (t, t), 1)
    suffix = jnp.where(row > col, 1.0, 0.0).astype(BF16)
    suffix2 = jnp.concatenate([suffix, suffix], axis=0)
    causal = col < row

    def visit(blocks):
        chains = []
        for qb, j, diagonal in blocks:
            start = pl.multiple_of(j * t, t)
            for hh in range(HEADS_PER_STEP):
                lanes = slice(hh * SB_HEAD_DIM, (hh + 1) * SB_HEAD_DIM)
                kj = k_ref[pl.ds(start, t), lanes]
                z = lax.dot_general(q_ref[qb * t:(qb + 1) * t, lanes], kj,
                                    (((1,), (1,)), ((), ())), preferred_element_type=F32)
                chains.append(dict(qb=qb, hh=hh, diagonal=diagonal, start=start, lanes=lanes, z=z))
        for c in chains:
            z = c.pop("z")
            log_beta = jnp.minimum(z, 0.0) - jnp.log(1.0 + jnp.exp(-jnp.abs(z)))
            lom = log_beta - z
            if c["diagonal"]:
                lom = jnp.where(causal, lom, 0.0)
            hi = lom.astype(BF16)
            lo = (lom - hi.astype(F32)).astype(BF16)
            c.update(log_beta=log_beta, first=lom[:, 0:1], split=jnp.concatenate([hi, lo], axis=1))
        for c in chains:
            c["after"] = _dot(c.pop("split"), suffix2)
        carries = {}
        for c in chains:
            key = (c["qb"], c["hh"])
            if key not in carries:
                carries[key] = carry_ref[key]
            after = c["after"]
            w = jnp.exp(c["log_beta"] + after + carries[key])
            if c["diagonal"]:
                w = jnp.where(causal, w, 0.0)
            vj = v_ref[pl.ds(c["start"], t), c["lanes"]]
            acc_ref[key] += _dot(w.astype(BF16), vj)
            carries[key] = carries[key] + after[:, 0:1] + c["first"]
        worst = {}
        for key, carry in carries.items():
            carry_ref[key] = carry
            top = jnp.max(carry)
            worst[key[0]] = jnp.maximum(worst[key[0]], top) if key[0] in worst else top
        return tuple(worst[qb] for qb in sorted(worst))

    acc_ref[...] = jnp.zeros_like(acc_ref)
    carry_ref[...] = jnp.zeros_like(carry_ref)

    diagonals = [(qb, first_block + qb, True) for qb in range(ATT_BLOCKS_PER_STEP)]
    previous = [(qb, first_block + qb - 1, False) for qb in range(ATT_BLOCKS_PER_STEP)]
    order = lambda blocks: sorted(blocks, key=lambda b: (b[0], not b[2]))
    worst = lax.cond(first_block > 0,
                     lambda: visit(order(diagonals + previous)),
                     lambda: visit(order(diagonals + previous[1:])))

    for qb in range(ATT_BLOCKS_PER_STEP):
        def more(state):
            j, worst = state
            return jnp.logical_and(j >= 0, worst > SB_EXIT_LOG)

        def step(state, qb=qb):
            j, _ = state
            return j - 1, visit([(qb, j, False)])[0]

        lax.while_loop(more, step, (first_block + qb - 2, worst[qb]))
        for hh in range(HEADS_PER_STEP):
            o_ref[qb * t:(qb + 1) * t, hh * SB_HEAD_DIM:(hh + 1) * SB_HEAD_DIM] = (
                acc_ref[qb, hh].astype(BF16))


def _attention(qkv, batch, seq):
    n = qkv.shape[0]
    t = ATT_TILE
    rows = ATT_BLOCKS_PER_STEP * t
    steps = seq // rows
    pairs = SB_HEADS // HEADS_PER_STEP
    return pl.pallas_call(
        _attn_kernel,
        grid=(batch, pairs, steps),
        in_specs=[
            pl.BlockSpec((rows, LANES), lambda b, p, i: (b * steps + i, p)),
            pl.BlockSpec((seq, LANES), lambda b, p, i: (b, pairs + p)),
            pl.BlockSpec((seq, LANES), lambda b, p, i: (b, 2 * pairs + p)),
        ],
        out_specs=pl.BlockSpec((rows, LANES), lambda b, p, i: (b * steps + i, p)),
        out_shape=jax.ShapeDtypeStruct((n, SB_WIDTH), BF16),
        scratch_shapes=[
            pltpu.VMEM((ATT_BLOCKS_PER_STEP, HEADS_PER_STEP, t, SB_HEAD_DIM), F32),
            pltpu.VMEM((ATT_BLOCKS_PER_STEP, HEADS_PER_STEP, t, 1), F32)],
        compiler_params=_params("arbitrary", "arbitrary", "arbitrary"),
        name="sb_attention",
    )(qkv, qkv, qkv)


def _gelu_tanh(x):
    c = math.sqrt(2.0 / math.pi)
    return 0.5 * x * (1.0 + jnp.tanh(c * (x + 0.044715 * (x * x * x))))


def _s5_kernel(u_ref, wb_ref, einv_re_ref, einv_im_ref, e_re_ref, e_im_ref,
               wc_ref, d_ref, wglu_ref, bglu_ref, o_ref,
               st_ref, bu_ref, xs_ref, cum_ref, hs_ref, y_ref):
    ell = SSM_CHUNK
    nb = SSM_BLOCK
    n_chunks = SSM_ROWS // ell

    @pl.when(pl.program_id(1) == 0)
    def _():
        st_ref[...] = jnp.zeros_like(st_ref)

    row = lax.broadcasted_iota(jnp.int32, (SCAN_ROWS, SCAN_ROWS), 0)
    col = lax.broadcasted_iota(jnp.int32, (SCAN_ROWS, SCAN_ROWS), 1)
    same_chunk = (row // ell) == (col // ell)
    prefix = jnp.where(jnp.logical_and(same_chunk, col <= row), 1.0, 0.0).astype(BF16)

    u = u_ref[...]
    u16 = u.astype(BF16)
    for sb in range(N_STATE // nb):
        states = slice(sb * nb, (sb + 1) * nb)
        chans = slice(sb * SSM_BLOCK_CH, (sb + 1) * SSM_BLOCK_CH)
        bu_ref[...] = _dot(u16[:, chans], wb_ref[sb])
        v_re = einv_re_ref[:, states]
        v_im = einv_im_ref[:, states]
        for c in range(n_chunks):
            rows = slice(c * ell, (c + 1) * ell)
            b_re = bu_ref[rows, :nb]
            b_im = bu_ref[rows, nb:]
            xs_ref[rows, :nb] = (b_re * v_re - b_im * v_im).astype(BF16)
            xs_ref[rows, nb:] = (b_re * v_im + b_im * v_re).astype(BF16)
        for r0 in range(0, SSM_ROWS, SCAN_ROWS):
            slab = slice(r0, r0 + SCAN_ROWS)
            cum_ref[slab, :] = _dot(prefix, xs_ref[slab, :])
        e_re = e_re_ref[:, states]
        e_im = e_im_ref[:, states]
        st_re = st_ref[0:1, states]
        st_im = st_ref[1:2, states]
        for c in range(n_chunks):
            rows = slice(c * ell, (c + 1) * ell)
            a_re = cum_ref[rows, :nb] + st_re
            a_im = cum_ref[rows, nb:] + st_im
            h_re = e_re * a_re - e_im * a_im
            h_im = e_re * a_im + e_im * a_re
            st_re = h_re[ell - 1:ell, :]
            st_im = h_im[ell - 1:ell, :]
            hs_ref[rows, :nb] = h_re.astype(BF16)
            hs_ref[rows, nb:] = h_im.astype(BF16)
        st_ref[0:1, states] = st_re
        st_ref[1:2, states] = st_im
        y_ref[:, chans] = _dot(hs_ref[...], wc_ref[sb])
    y = _gelu_tanh(y_ref[...] + d_ref[...] * u)
    gate = _dot(y.astype(BF16), wglu_ref[...]) + bglu_ref[...]
    o_ref[...] = (y * _sigmoid(gate)).astype(BF16)


def _s5_tables(a_re, a_im, log_dt, b_re, b_im, c_re, c_im):
    g, p, cg = SSM_GROUPS, SSM_STATE, SSM_GROUP
    gb = SSM_BLOCK // p
    nblk = g // gb
    dt = jnp.exp(log_dt)[:, None]
    ld_re, ld_im = a_re * dt, a_im * dt

    def cexp(re, im):
        mag = jnp.exp(re)
        return mag * jnp.cos(im), mag * jnp.sin(im)

    lb_re, lb_im = cexp(ld_re, ld_im)
    den = a_re * a_re + a_im * a_im
    f_re = ((lb_re - 1.0) * a_re + lb_im * a_im) / den
    f_im = (lb_im * a_re - (lb_re - 1.0) * a_im) / den
    bb_re = f_re[..., None] * b_re - f_im[..., None] * b_im
    bb_im = f_re[..., None] * b_im + f_im[..., None] * b_re
    eye = jnp.eye(gb, dtype=F32)

    def in_map(b):
        blocks = jnp.einsum('bgpc,gh->bgchp', b.reshape(nblk, gb, p, cg), eye)
        return blocks.reshape(nblk, gb * cg, gb * p)

    def out_map(c):
        blocks = jnp.einsum('bgcp,gh->bgphc', c.reshape(nblk, gb, cg, p), eye)
        return blocks.reshape(nblk, gb * p, gb * cg)

    wb = jnp.concatenate([in_map(bb_re), in_map(bb_im)], axis=2).astype(BF16)
    wc = jnp.concatenate([out_map(c_re), -out_map(c_im)], axis=1).astype(BF16)
    steps = jnp.arange(1, SSM_CHUNK + 1, dtype=F32)[:, None, None]
    e_re, e_im = cexp(ld_re[None] * steps, ld_im[None] * steps)
    v_re, v_im = cexp(-ld_re[None] * steps, -ld_im[None] * steps)
    flat = lambda a: a.reshape(SSM_CHUNK, g * p)
    return wb, flat(v_re), flat(v_im), flat(e_re), flat(e_im), wc


def _s5(u, tables, d_skip, w_glu_bf16, b_glu, batch, seq):
    n = u.shape[0]
    wb, einv_re, einv_im, e_re, e_im, wc = tables
    steps = seq // SSM_ROWS
    tab = (SSM_CHUNK, N_STATE)
    return pl.pallas_call(
        _s5_kernel,
        grid=(batch, steps),
        in_specs=[
            pl.BlockSpec((SSM_ROWS, SSM_WIDTH), lambda b, i: (b * steps + i, 0)),
            _resident(wb.shape),
            _resident(tab), _resident(tab), _resident(tab), _resident(tab),
            _resident(wc.shape),
            _resident((1, SSM_WIDTH)),
            _resident((SSM_WIDTH, SSM_WIDTH)),
            _resident((1, SSM_WIDTH)),
        ],
        out_specs=pl.BlockSpec((SSM_ROWS, SSM_WIDTH), lambda b, i: (b * steps + i, 0)),
        out_shape=jax.ShapeDtypeStruct((n, SSM_WIDTH), BF16),
        scratch_shapes=[
            pltpu.VMEM((2, N_STATE), F32),
            pltpu.VMEM((SSM_ROWS, 2 * SSM_BLOCK), F32),
            pltpu.VMEM((SSM_ROWS, 2 * SSM_BLOCK), BF16),
            pltpu.VMEM((SSM_ROWS, 2 * SSM_BLOCK), F32),
            pltpu.VMEM((SSM_ROWS, 2 * SSM_BLOCK), BF16),
            pltpu.VMEM((SSM_ROWS, SSM_WIDTH), F32),
        ],
        compiler_params=_params("arbitrary", "arbitrary"),
        name="s5_scan",
    )(u, wb, einv_re, einv_im, e_re, e_im, wc,
      d_skip.reshape(1, SSM_WIDTH), w_glu_bf16, b_glu.reshape(1, SSM_WIDTH))


def _mixout_kernel(x_ref, att_ref, s5_ref, g_ref, mod_ref, wsb_ref, wssm_ref,
                   wout_ref, lng_ref, lnb_ref, o_ref, *, alpha):
    y_sb = _dot(att_ref[...], wsb_ref[...])
    y_ssm = _dot(s5_ref[...], wssm_ref[...])
    g_sb = g_ref[:, :D_MODEL].astype(F32)
    g_ssm = g_ref[:, D_MODEL:].astype(F32)
    merged = _sigmoid(g_sb) * y_sb + _sigmoid(g_ssm) * y_ssm
    y = _dot(merged.astype(BF16), wout_ref[...])
    gate = mod_ref[2:3, :]
    r = alpha * x_ref[...] + (1.0 + gate) * y
    o_ref[...] = _normalize(r) * lng_ref[...] + lnb_ref[...]


def _mix_out(x2d, att, s5, g, mod, w_sb_up, w_ssm_up, w_out, ln_g, ln_b, seq, alpha):
    n = x2d.shape[0]
    tiles_per_seq = seq // ROW_TILE
    rows = lambda width: pl.BlockSpec((ROW_TILE, width), lambda i: (i, 0))
    return pl.pallas_call(
        functools.partial(_mixout_kernel, alpha=alpha),
        grid=(n // ROW_TILE,),
        in_specs=[
            rows(D_MODEL), rows(SB_WIDTH), rows(SSM_WIDTH), rows(2 * D_MODEL),
            pl.BlockSpec((None, N_MOD, D_MODEL), lambda i: (i // tiles_per_seq, 0, 0)),
            _resident((SB_WIDTH, D_MODEL)),
            _resident((SSM_WIDTH, D_MODEL)),
            _resident((D_MODEL, D_MODEL)),
            _resident((1, D_MODEL)),
            _resident((1, D_MODEL)),
        ],
        out_specs=rows(D_MODEL),
        out_shape=jax.ShapeDtypeStruct((n, D_MODEL), F32),
        compiler_params=_params("arbitrary"),
        name="mix_out",
    )(x2d, att, s5, g, mod, w_sb_up, w_ssm_up, w_out,
      ln_g.reshape(1, D_MODEL), ln_b.reshape(1, D_MODEL))


def _ffn_kernel(x_ref, mod_ref, win_ref, wout_ref, lng_ref, lnb_ref, o_ref,
                h_ref, acc_ref, *, alpha):
    x = x_ref[...]
    shift = mod_ref[3:4, :]
    scale = mod_ref[4:5, :]
    gate_mod = mod_ref[5:6, :]
    h_ref[...] = (_normalize(x) * (1.0 + scale) + shift).astype(BF16)
    h = h_ref[...]
    for j in range(FFN_HIDDEN // FFN_CHUNK):
        cols = slice(j * FFN_CHUNK, (j + 1) * FFN_CHUNK)
        up_cols = slice(FFN_HIDDEN + j * FFN_CHUNK, FFN_HIDDEN + (j + 1) * FFN_CHUNK)
        gate = _dot(h, win_ref[:, cols])
        up = _dot(h, win_ref[:, up_cols])
        act = (gate * _sigmoid(gate) * up).astype(BF16)
        part = _dot(act, wout_ref[cols, :])
        if j == 0:
            acc_ref[...] = part
        else:
            acc_ref[...] += part
    r = alpha * x + (1.0 + gate_mod) * acc_ref[...]
    o_ref[...] = _normalize(r) * lng_ref[...] + lnb_ref[...]


def _ffn(x2d, mod, w_ffn_in, w_ffn_out, ln_g, ln_b, seq, alpha):
    n = x2d.shape[0]
    tiles_per_seq = seq // ROW_TILE
    return pl.pallas_call(
        functools.partial(_ffn_kernel, alpha=alpha),
        grid=(n // ROW_TILE,),
        in_specs=[
            pl.BlockSpec((ROW_TILE, D_MODEL), lambda i: (i, 0)),
            pl.BlockSpec((None, N_MOD, D_MODEL), lambda i: (i // tiles_per_seq, 0, 0)),
            _resident((D_MODEL, 2 * FFN_HIDDEN)),
            _resident((FFN_HIDDEN, D_MODEL)),
            _resident((1, D_MODEL)),
            _resident((1, D_MODEL)),
        ],
        out_specs=pl.BlockSpec((ROW_TILE, D_MODEL), lambda i: (i, 0)),
        out_shape=jax.ShapeDtypeStruct((n, D_MODEL), F32),
        scratch_shapes=[pltpu.VMEM((ROW_TILE, D_MODEL), BF16),
                        pltpu.VMEM((ROW_TILE, D_MODEL), F32)],
        compiler_params=_params("arbitrary"),
        name="ffn",
    )(x2d, mod, w_ffn_in, w_ffn_out, ln_g.reshape(1, D_MODEL), ln_b.reshape(1, D_MODEL))


def kernel(x, c, w_ada, b_ada, w_in, w_sb_up, ssm_a_re, ssm_a_im, ssm_log_dt,
           ssm_b_re, ssm_b_im, ssm_c_re, ssm_c_im, ssm_d, w_glu, b_glu,
           w_ssm_up, w_out, ln1_g, ln1_b, w_ffn_in, w_ffn_out, ln2_g, ln2_b):
    batch, seq, d = x.shape
    depth = w_in.shape[0]
    assert d == D_MODEL and seq % max(ROW_TILE, ATT_TILE * ATT_BLOCKS_PER_STEP, SSM_ROWS) == 0
    alpha = (2 * depth) ** 0.25
    mod = _modulation(c, w_ada, b_ada)
    x2d = x.reshape(batch * seq, d)
    for l in range(depth):
        qkv, u, g = _in_projection(x2d, mod[l], w_in[l].astype(BF16), seq)
        att = _attention(qkv, batch, seq)
        tables = _s5_tables(ssm_a_re[l], ssm_a_im[l], ssm_log_dt[l], ssm_b_re[l],
                            ssm_b_im[l], ssm_c_re[l], ssm_c_im[l])
        s5 = _s5(u, tables, ssm_d[l], w_glu[l].astype(BF16), b_glu[l], batch, seq)
        x2d = _mix_out(x2d, att, s5, g, mod[l], w_sb_up[l].astype(BF16),
                       w_ssm_up[l].astype(BF16), w_out[l].astype(BF16),
                       ln1_g[l], ln1_b[l], seq, alpha)
        x2d = _ffn(x2d, mod[l], w_ffn_in[l].astype(BF16), w_ffn_out[l].astype(BF16),
                   ln2_g[l], ln2_b[l], seq, alpha)
    return x2d.reshape(batch, seq, d)
```

```python
import functools
import math

import jax
import jax.numpy as jnp
from jax import lax
from jax.experimental import pallas as pl
from jax.experimental.pallas import tpu as pltpu

F32 = jnp.float32
BF16 = jnp.bfloat16

D_MODEL = 1024
SB_HEADS = 8
SB_HEAD_DIM = 64
SB_WIDTH = SB_HEADS * SB_HEAD_DIM
SSM_WIDTH = D_MODEL // 2
SSM_GROUP = 16
SSM_GROUPS = SSM_WIDTH // SSM_GROUP
SSM_STATE = 64
N_STATE = SSM_GROUPS * SSM_STATE
FFN_HIDDEN = 2816
IN_COLS = 3 * SB_WIDTH + SSM_WIDTH + 2 * D_MODEL
N_MOD = 6
LN_EPS = 1e-5

LANES = 128
SUBLANES = 8
VMEM_LIMIT_BYTES = 56 * 1024 * 1024

ROW_TILE = 512
ATT_TILE = 256
ATT_BLOCKS_PER_STEP = 2
SSM_CHUNK = 64
SSM_ROWS = 512
SSM_BLOCK = 512
SSM_BLOCK_CH = SSM_BLOCK // SSM_STATE * SSM_GROUP
SCAN_ROWS = 256
FFN_CHUNK = 256
HEADS_PER_STEP = LANES // SB_HEAD_DIM
SB_EXIT_LOG = -105.0


def _sigmoid(x):
    return 1.0 / (1.0 + jnp.exp(-x))


def _normalize(x):
    mu = jnp.mean(x, axis=-1, keepdims=True)
    xc = x - mu
    var = jnp.mean(xc * xc, axis=-1, keepdims=True)
    return xc * lax.rsqrt(var + LN_EPS)


def _dot(a, b):
    return jnp.dot(a, b, preferred_element_type=F32)


def _resident(shape):
    zeros = (0,) * len(shape)
    return pl.BlockSpec(shape, lambda *_: zeros, pipeline_mode=pl.Buffered(1))


def _layer_resident(shape, layer):
    index = (layer,) + (0,) * len(shape)
    return pl.BlockSpec((None,) + tuple(shape), lambda *_: index, pipeline_mode=pl.Buffered(1))


def _layer_mod(layer, tiles_per_seq):
    return pl.BlockSpec((None, None, N_MOD, D_MODEL),
                        lambda i: (layer, i // tiles_per_seq, 0, 0))


def _params(*semantics):
    return pltpu.CompilerParams(dimension_semantics=semantics,
                                vmem_limit_bytes=VMEM_LIMIT_BYTES)


def _mod_kernel(c_ref, w_ref, b_ref, o_ref):
    c = c_ref[...]
    c_act = c * _sigmoid(c)
    o_ref[...] = jnp.dot(c_act, w_ref[...], preferred_element_type=F32,
                         precision=lax.Precision.HIGHEST) + b_ref[...]


def _modulation(c, w_ada, b_ada):
    depth = w_ada.shape[0]
    batch = c.shape[0]
    rows = -(-batch // SUBLANES) * SUBLANES
    c_pad = jnp.pad(c, ((0, rows - batch), (0, 0)))
    cols = N_MOD * D_MODEL
    out = pl.pallas_call(
        _mod_kernel,
        grid=(depth, N_MOD),
        in_specs=[
            pl.BlockSpec((rows, D_MODEL), lambda l, j: (0, 0)),
            pl.BlockSpec((None, D_MODEL, D_MODEL), lambda l, j: (l, 0, j)),
            pl.BlockSpec((None, 1, D_MODEL), lambda l, j: (l, 0, j)),
        ],
        out_specs=pl.BlockSpec((None, rows, D_MODEL), lambda l, j: (l, 0, j)),
        out_shape=jax.ShapeDtypeStruct((depth, rows, cols), F32),
        compiler_params=_params("arbitrary", "arbitrary"),
        name="adaln_mod",
    )(c_pad, w_ada, b_ada.reshape(depth, 1, cols))
    return out[:, :batch].reshape(depth, batch, N_MOD, D_MODEL)


def _inproj_kernel(x_ref, mod_ref, w_ref, qkv_ref, u_ref, g_ref, h_ref):
    shift = mod_ref[0:1, :]
    scale = mod_ref[1:2, :]
    h_ref[...] = (_normalize(x_ref[...]) * (1.0 + scale) + shift).astype(BF16)
    h = h_ref[...]
    w = SB_WIDTH
    qkv_ref[:, 0:w] = (_dot(h, w_ref[:, 0:w]) * (1.0 / math.sqrt(SB_HEAD_DIM))).astype(BF16)
    qkv_ref[:, w:2 * w] = _dot(h, w_ref[:, w:2 * w]).astype(BF16)
    qkv_ref[:, 2 * w:3 * w] = _dot(h, w_ref[:, 2 * w:3 * w]).astype(BF16)
    u_ref[...] = _dot(h, w_ref[:, 3 * w:3 * w + SSM_WIDTH])
    g0 = 3 * w + SSM_WIDTH
    for j in range(2 * D_MODEL // w):
        g_ref[:, j * w:(j + 1) * w] = _dot(h, w_ref[:, g0 + j * w:g0 + (j + 1) * w]).astype(BF16)


def _in_projection(x2d, mod, w_in_bf16, layer, seq):
    n = x2d.shape[0]
    tiles_per_seq = seq // ROW_TILE
    return pl.pallas_call(
        _inproj_kernel,
        grid=(n // ROW_TILE,),
        in_specs=[
            pl.BlockSpec((ROW_TILE, D_MODEL), lambda i: (i, 0)),
            _layer_mod(layer, tiles_per_seq),
            _layer_resident((D_MODEL, IN_COLS), layer),
        ],
        out_specs=[
            pl.BlockSpec((ROW_TILE, 3 * SB_WIDTH), lambda i: (i, 0)),
            pl.BlockSpec((ROW_TILE, SSM_WIDTH), lambda i: (i, 0)),
            pl.BlockSpec((ROW_TILE, 2 * D_MODEL), lambda i: (i, 0)),
        ],
        out_shape=[
            jax.ShapeDtypeStruct((n, 3 * SB_WIDTH), BF16),
            jax.ShapeDtypeStruct((n, SSM_WIDTH), F32),
            jax.ShapeDtypeStruct((n, 2 * D_MODEL), BF16),
        ],
        scratch_shapes=[pltpu.VMEM((ROW_TILE, D_MODEL), BF16)],
        compiler_params=_params("arbitrary"),
        name="in_proj",
    )(x2d, mod, w_in_bf16)


def _attn_kernel(q_ref, k_ref, v_ref, o_ref, acc_ref, carry_ref):
    t = ATT_TILE
    first_block = pl.program_id(2) * ATT_BLOCKS_PER_STEP
    row = lax.broadcasted_iota(jnp.int32, (t, t), 0)
    col = lax.broadcasted_iota(jnp.int32, (t, t), 1)
    suffix = jnp.where(row > col, 1.0, 0.0).astype(BF16)
    suffix2 = jnp.concatenate([suffix, suffix], axis=0)
    causal = col < row

    def visit(blocks):
        chains = []
        for qb, j, diagonal in blocks:
            start = pl.multiple_of(j * t, t)
            for hh in range(HEADS_PER_STEP):
                lanes = slice(hh * SB_HEAD_DIM, (hh + 1) * SB_HEAD_DIM)
                kj = k_ref[pl.ds(start, t), lanes]
                z = lax.dot_general(q_ref[qb * t:(qb + 1) * t, lanes], kj,
                                    (((1,), (1,)), ((), ())), preferred_element_type=F32)
                chains.append(dict(qb=qb, hh=hh, diagonal=diagonal, start=start, lanes=lanes, z=z))
        for c in chains:
            z = c.pop("z")
            log_beta = jnp.minimum(z, 0.0) - jnp.log(1.0 + jnp.exp(-jnp.abs(z)))
            lom = log_beta - z
            if c["diagonal"]:
                lom = jnp.where(causal, lom, 0.0)
            hi = lom.astype(BF16)
            lo = (lom - hi.astype(F32)).astype(BF16)
            c.update(log_beta=log_beta, first=lom[:, 0:1], split=jnp.concatenate([hi, lo], axis=1))
        for c in chains:
            c["after"] = _dot(c.pop("split"), suffix2)
        carries = {}
        for c in chains:
            key = (c["qb"], c["hh"])
            if key not in carries:
                carries[key] = carry_ref[key]
            after = c["after"]
            w = jnp.exp(c["log_beta"] + after + carries[key])
            if c["diagonal"]:
                w = jnp.where(causal, w, 0.0)
            vj = v_ref[pl.ds(c["start"], t), c["lanes"]]
            acc_ref[key] += _dot(w.astype(BF16), vj)
            carries[key] = carries[key] + after[:, 0:1] + c["first"]
        worst = {}
        for key, carry in carries.items():
            carry_ref[key] = carry
            top = jnp.max(carry)
            worst[key[0]] = jnp.maximum(worst[key[0]], top) if key[0] in worst else top
        return tuple(worst[qb] for qb in sorted(worst))

    acc_ref[...] = jnp.zeros_like(acc_ref)
    carry_ref[...] = jnp.zeros_like(carry_ref)

    diagonals = [(qb, first_block + qb, True) for qb in range(ATT_BLOCKS_PER_STEP)]
    previous = [(qb, first_block + qb - 1, False) for qb in range(ATT_BLOCKS_PER_STEP)]
    order = lambda blocks: sorted(blocks, key=lambda b: (b[0], not b[2]))
    worst = lax.cond(first_block > 0,
                     lambda: visit(order(diagonals + previous)),
                     lambda: visit(order(diagonals + previous[1:])))

    for qb in range(ATT_BLOCKS_PER_STEP):
        def more(state):
            j, worst = state
            return jnp.logical_and(j >= 0, worst > SB_EXIT_LOG)

        def step(state, qb=qb):
            j, _ = state
            return j - 1, visit([(qb, j, False)])[0]

        lax.while_loop(more, step, (first_block + qb - 2, worst[qb]))
        for hh in range(HEADS_PER_STEP):
            o_ref[qb * t:(qb + 1) * t, hh * SB_HEAD_DIM:(hh + 1) * SB_HEAD_DIM] = (
                acc_ref[qb, hh].astype(BF16))


def _attention(qkv, batch, seq):
    n = qkv.shape[0]
    t = ATT_TILE
    rows = ATT_BLOCKS_PER_STEP * t
    steps = seq // rows
    pairs = SB_HEADS // HEADS_PER_STEP
    return pl.pallas_call(
        _attn_kernel,
        grid=(batch, pairs, steps),
        in_specs=[
            pl.BlockSpec((rows, LANES), lambda b, p, i: (b * steps + i, p)),
            pl.BlockSpec((seq, LANES), lambda b, p, i: (b, pairs + p)),
            pl.BlockSpec((seq, LANES), lambda b, p, i: (b, 2 * pairs + p)),
        ],
        out_specs=pl.BlockSpec((rows, LANES), lambda b, p, i: (b * steps + i, p)),
        out_shape=jax.ShapeDtypeStruct((n, SB_WIDTH), BF16),
        scratch_shapes=[
            pltpu.VMEM((ATT_BLOCKS_PER_STEP, HEADS_PER_STEP, t, SB_HEAD_DIM), F32),
            pltpu.VMEM((ATT_BLOCKS_PER_STEP, HEADS_PER_STEP, t, 1), F32)],
        compiler_params=_params("arbitrary", "arbitrary", "arbitrary"),
        name="sb_attention",
    )(qkv, qkv, qkv)


def _gelu_tanh(x):
    c = math.sqrt(2.0 / math.pi)
    return 0.5 * x * (1.0 + jnp.tanh(c * (x + 0.044715 * (x * x * x))))


def _s5_kernel(u_ref, wb_ref, einv_re_ref, einv_im_ref, e_re_ref, e_im_ref,
               wc_ref, d_ref, wglu_ref, bglu_ref, o_ref,
               st_ref, bu_ref, xs_ref, cum_ref, hs_ref, y_ref):
    ell = SSM_CHUNK
    nb = SSM_BLOCK
    n_chunks = SSM_ROWS // ell

    @pl.when(pl.program_id(1) == 0)
    def _():
        st_ref[...] = jnp.zeros_like(st_ref)

    row = lax.broadcasted_iota(jnp.int32, (SCAN_ROWS, SCAN_ROWS), 0)
    col = lax.broadcasted_iota(jnp.int32, (SCAN_ROWS, SCAN_ROWS), 1)
    same_chunk = (row // ell) == (col // ell)
    prefix = jnp.where(jnp.logical_and(same_chunk, col <= row), 1.0, 0.0).astype(BF16)

    u = u_ref[...]
    u16 = u.astype(BF16)
    blocks = range(N_STATE // nb)
    states = [slice(sb * nb, (sb + 1) * nb) for sb in blocks]
    chans = [slice(sb * SSM_BLOCK_CH, (sb + 1) * SSM_BLOCK_CH) for sb in blocks]
    for sb in blocks:
        bu_ref[sb] = _dot(u16[:, chans[sb]], wb_ref[sb])
    for sb in blocks:
        v_re = einv_re_ref[:, states[sb]]
        v_im = einv_im_ref[:, states[sb]]
        for c in range(n_chunks):
            rows = slice(c * ell, (c + 1) * ell)
            b_re = bu_ref[sb, rows, :nb]
            b_im = bu_ref[sb, rows, nb:]
            xs_ref[sb, rows, :nb] = (b_re * v_re - b_im * v_im).astype(BF16)
            xs_ref[sb, rows, nb:] = (b_re * v_im + b_im * v_re).astype(BF16)
    for sb in blocks:
        for r0 in range(0, SSM_ROWS, SCAN_ROWS):
            slab = slice(r0, r0 + SCAN_ROWS)
            cum_ref[sb, slab, :] = _dot(prefix, xs_ref[sb, slab, :])
    for sb in blocks:
        e_re = e_re_ref[:, states[sb]]
        e_im = e_im_ref[:, states[sb]]
        st_re = st_ref[0:1, states[sb]]
        st_im = st_ref[1:2, states[sb]]
        for c in range(n_chunks):
            rows = slice(c * ell, (c + 1) * ell)
            a_re = cum_ref[sb, rows, :nb] + st_re
            a_im = cum_ref[sb, rows, nb:] + st_im
            h_re = e_re * a_re - e_im * a_im
            h_im = e_re * a_im + e_im * a_re
            st_re = h_re[ell - 1:ell, :]
            st_im = h_im[ell - 1:ell, :]
            hs_ref[sb, rows, :nb] = h_re.astype(BF16)
            hs_ref[sb, rows, nb:] = h_im.astype(BF16)
        st_ref[0:1, states[sb]] = st_re
        st_ref[1:2, states[sb]] = st_im
    for sb in blocks:
        y_ref[:, chans[sb]] = _dot(hs_ref[sb], wc_ref[sb])
    y = _gelu_tanh(y_ref[...] + d_ref[...] * u)
    gate = _dot(y.astype(BF16), wglu_ref[...]) + bglu_ref[...]
    o_ref[...] = (y * _sigmoid(gate)).astype(BF16)


def _s5_tables(a_re, a_im, log_dt, b_re, b_im, c_re, c_im):
    g, p, cg = SSM_GROUPS, SSM_STATE, SSM_GROUP
    gb = SSM_BLOCK // p
    nblk = g // gb
    dt = jnp.exp(log_dt)[:, None]
    ld_re, ld_im = a_re * dt, a_im * dt

    def cexp(re, im):
        mag = jnp.exp(re)
        return mag * jnp.cos(im), mag * jnp.sin(im)

    lb_re, lb_im = cexp(ld_re, ld_im)
    den = a_re * a_re + a_im * a_im
    f_re = ((lb_re - 1.0) * a_re + lb_im * a_im) / den
    f_im = (lb_im * a_re - (lb_re - 1.0) * a_im) / den
    bb_re = f_re[..., None] * b_re - f_im[..., None] * b_im
    bb_im = f_re[..., None] * b_im + f_im[..., None] * b_re
    eye = jnp.eye(gb, dtype=F32)

    def in_map(b):
        blocks = jnp.einsum('bgpc,gh->bgchp', b.reshape(nblk, gb, p, cg), eye)
        return blocks.reshape(nblk, gb * cg, gb * p)

    def out_map(c):
        blocks = jnp.einsum('bgcp,gh->bgphc', c.reshape(nblk, gb, cg, p), eye)
        return blocks.reshape(nblk, gb * p, gb * cg)

    wb = jnp.concatenate([in_map(bb_re), in_map(bb_im)], axis=2).astype(BF16)
    wc = jnp.concatenate([out_map(c_re), -out_map(c_im)], axis=1).astype(BF16)
    steps = jnp.arange(1, SSM_CHUNK + 1, dtype=F32)[:, None, None]
    e_re, e_im = cexp(ld_re[None] * steps, ld_im[None] * steps)
    v_re, v_im = cexp(-ld_re[None] * steps, -ld_im[None] * steps)
    flat = lambda a: a.reshape(SSM_CHUNK, g * p)
    return wb, flat(v_re), flat(v_im), flat(e_re), flat(e_im), wc


def _s5(u, tables, d_skip, w_glu_bf16, b_glu, layer, batch, seq):
    n = u.shape[0]
    wb, einv_re, einv_im, e_re, e_im, wc = tables
    steps = seq // SSM_ROWS
    nblk = N_STATE // SSM_BLOCK
    tab = (SSM_CHUNK, N_STATE)
    return pl.pallas_call(
        _s5_kernel,
        grid=(batch, steps),
        in_specs=[
            pl.BlockSpec((SSM_ROWS, SSM_WIDTH), lambda b, i: (b * steps + i, 0)),
            _resident(wb.shape),
            _resident(tab), _resident(tab), _resident(tab), _resident(tab),
            _resident(wc.shape),
            _layer_resident((1, SSM_WIDTH), layer),
            _layer_resident((SSM_WIDTH, SSM_WIDTH), layer),
            _layer_resident((1, SSM_WIDTH), layer),
        ],
        out_specs=pl.BlockSpec((SSM_ROWS, SSM_WIDTH), lambda b, i: (b * steps + i, 0)),
        out_shape=jax.ShapeDtypeStruct((n, SSM_WIDTH), BF16),
        scratch_shapes=[
            pltpu.VMEM((2, N_STATE), F32),
            pltpu.VMEM((nblk, SSM_ROWS, 2 * SSM_BLOCK), F32),
            pltpu.VMEM((nblk, SSM_ROWS, 2 * SSM_BLOCK), BF16),
            pltpu.VMEM((nblk, SSM_ROWS, 2 * SSM_BLOCK), F32),
            pltpu.VMEM((nblk, SSM_ROWS, 2 * SSM_BLOCK), BF16),
            pltpu.VMEM((SSM_ROWS, SSM_WIDTH), F32),
        ],
        compiler_params=_params("arbitrary", "arbitrary"),
        name="s5_scan",
    )(u, wb, einv_re, einv_im, e_re, e_im, wc,
      d_skip.reshape(-1, 1, SSM_WIDTH), w_glu_bf16, b_glu.reshape(-1, 1, SSM_WIDTH))


def _mixout_kernel(x_ref, att_ref, s5_ref, g_ref, mod_ref, wsb_ref, wssm_ref,
                   wout_ref, lng_ref, lnb_ref, o_ref, *, alpha):
    y_sb = _dot(att_ref[...], wsb_ref[...])
    y_ssm = _dot(s5_ref[...], wssm_ref[...])
    g_sb = g_ref[:, :D_MODEL].astype(F32)
    g_ssm = g_ref[:, D_MODEL:].astype(F32)
    merged = _sigmoid(g_sb) * y_sb + _sigmoid(g_ssm) * y_ssm
    y = _dot(merged.astype(BF16), wout_ref[...])
    gate = mod_ref[2:3, :]
    r = alpha * x_ref[...] + (1.0 + gate) * y
    o_ref[...] = _normalize(r) * lng_ref[...] + lnb_ref[...]


def _mix_out(x2d, att, s5, g, mod, w_sb_up, w_ssm_up, w_out, ln_g, ln_b, layer, seq, alpha):
    n = x2d.shape[0]
    tiles_per_seq = seq // ROW_TILE
    rows = lambda width: pl.BlockSpec((ROW_TILE, width), lambda i: (i, 0))
    return pl.pallas_call(
        functools.partial(_mixout_kernel, alpha=alpha),
        grid=(n // ROW_TILE,),
        in_specs=[
            rows(D_MODEL), rows(SB_WIDTH), rows(SSM_WIDTH), rows(2 * D_MODEL),
            _layer_mod(layer, tiles_per_seq),
            _layer_resident((SB_WIDTH, D_MODEL), layer),
            _layer_resident((SSM_WIDTH, D_MODEL), layer),
            _layer_resident((D_MODEL, D_MODEL), layer),
            _layer_resident((1, D_MODEL), layer),
            _layer_resident((1, D_MODEL), layer),
        ],
        out_specs=rows(D_MODEL),
        out_shape=jax.ShapeDtypeStruct((n, D_MODEL), F32),
        compiler_params=_params("arbitrary"),
        name="mix_out",
    )(x2d, att, s5, g, mod, w_sb_up, w_ssm_up, w_out,
      ln_g.reshape(-1, 1, D_MODEL), ln_b.reshape(-1, 1, D_MODEL))


def _ffn_kernel(x_ref, mod_ref, win_ref, wout_ref, lng_ref, lnb_ref, o_ref,
                h_ref, acc_ref, *, alpha):
    x = x_ref[...]
    shift = mod_ref[3:4, :]
    scale = mod_ref[4:5, :]
    gate_mod = mod_ref[5:6, :]
    h_ref[...] = (_normalize(x) * (1.0 + scale) + shift).astype(BF16)
    h = h_ref[...]
    for j in range(FFN_HIDDEN // FFN_CHUNK):
        cols = slice(j * FFN_CHUNK, (j + 1) * FFN_CHUNK)
        up_cols = slice(FFN_HIDDEN + j * FFN_CHUNK, FFN_HIDDEN + (j + 1) * FFN_CHUNK)
        gate = _dot(h, win_ref[:, cols])
        up = _dot(h, win_ref[:, up_cols])
        act = (gate * _sigmoid(gate) * up).astype(BF16)
        part = _dot(act, wout_ref[cols, :])
        if j == 0:
            acc_ref[...] = part
        else:
            acc_ref[...] += part
    r = alpha * x + (1.0 + gate_mod) * acc_ref[...]
    o_ref[...] = _normalize(r) * lng_ref[...] + lnb_ref[...]


def _ffn(x2d, mod, w_ffn_in, w_ffn_out, ln_g, ln_b, layer, seq, alpha):
    n = x2d.shape[0]
    tiles_per_seq = seq // ROW_TILE
    return pl.pallas_call(
        functools.partial(_ffn_kernel, alpha=alpha),
        grid=(n // ROW_TILE,),
        in_specs=[
            pl.BlockSpec((ROW_TILE, D_MODEL), lambda i: (i, 0)),
            _layer_mod(layer, tiles_per_seq),
            _layer_resident((D_MODEL, 2 * FFN_HIDDEN), layer),
            _layer_resident((FFN_HIDDEN, D_MODEL), layer),
            _layer_resident((1, D_MODEL), layer),
            _layer_resident((1, D_MODEL), layer),
        ],
        out_specs=pl.BlockSpec((ROW_TILE, D_MODEL), lambda i: (i, 0)),
        out_shape=jax.ShapeDtypeStruct((n, D_MODEL), F32),
        scratch_shapes=[pltpu.VMEM((ROW_TILE, D_MODEL), BF16),
                        pltpu.VMEM((ROW_TILE, D_MODEL), F32)],
        compiler_params=_params("arbitrary"),
        name="ffn",
    )(x2d, mod, w_ffn_in, w_ffn_out, ln_g.reshape(-1, 1, D_MODEL), ln_b.reshape(-1, 1, D_MODEL))


def kernel(x, c, w_ada, b_ada, w_in, w_sb_up, ssm_a_re, ssm_a_im, ssm_log_dt,
           ssm_b_re, ssm_b_im, ssm_c_re, ssm_c_im, ssm_d, w_glu, b_glu,
           w_ssm_up, w_out, ln1_g, ln1_b, w_ffn_in, w_ffn_out, ln2_g, ln2_b):
    batch, seq, d = x.shape
    depth = w_in.shape[0]
    assert d == D_MODEL and seq % max(ROW_TILE, ATT_TILE * ATT_BLOCKS_PER_STEP, SSM_ROWS) == 0
    alpha = (2 * depth) ** 0.25
    mod = _modulation(c, w_ada, b_ada)
    x2d = x.reshape(batch * seq, d)
    w_in, w_sb_up, w_ssm_up, w_out, w_glu, w_ffn_in, w_ffn_out = (
        w.astype(BF16) for w in (w_in, w_sb_up, w_ssm_up, w_out, w_glu, w_ffn_in, w_ffn_out))
    for l in range(depth):
        qkv, u, g = _in_projection(x2d, mod, w_in, l, seq)
        att = _attention(qkv, batch, seq)
        tables = _s5_tables(ssm_a_re[l], ssm_a_im[l], ssm_log_dt[l], ssm_b_re[l],
                            ssm_b_im[l], ssm_c_re[l], ssm_c_im[l])
        s5 = _s5(u, tables, ssm_d, w_glu, b_glu, l, batch, seq)
        x2d = _mix_out(x2d, att, s5, g, mod, w_sb_up, w_ssm_up, w_out, ln1_g, ln1_b,
                       l, seq, alpha)
        x2d = _ffn(x2d, mod, w_ffn_in, w_ffn_out, ln2_g, ln2_b, l, seq, alpha)
    return x2d.reshape(batch, seq, d)
```

```python
import functools
import math

import jax
import jax.numpy as jnp
from jax import lax
from jax.experimental import pallas as pl
from jax.experimental.pallas import tpu as pltpu

F32 = jnp.float32
BF16 = jnp.bfloat16

D_MODEL = 1024
SB_HEADS = 8
SB_HEAD_DIM = 64
SB_WIDTH = SB_HEADS * SB_HEAD_DIM
SSM_WIDTH = D_MODEL // 2
SSM_GROUP = 16
SSM_GROUPS = SSM_WIDTH // SSM_GROUP
SSM_STATE = 64
N_STATE = SSM_GROUPS * SSM_STATE
FFN_HIDDEN = 2816
IN_COLS = 3 * SB_WIDTH + SSM_WIDTH + 2 * D_MODEL
N_MOD = 6
LN_EPS = 1e-5

LANES = 128
SUBLANES = 8
VMEM_LIMIT_BYTES = 56 * 1024 * 1024

ROW_TILE = 512
MIX_SLAB = 256
ATT_TILE = 256
ATT_BLOCKS_PER_STEP = 2
SSM_CHUNK = 16
SSM_ROWS = 512
SSM_BLOCK = 512
SSM_BLOCK_CH = SSM_BLOCK // SSM_STATE * SSM_GROUP
SCAN_ROWS = 256
FFN_CHUNK = 256
HEADS_PER_STEP = LANES // SB_HEAD_DIM
SB_EXIT_LOG = -105.0


def _sigmoid(x):
    return 1.0 / (1.0 + jnp.exp(-x))


def _normalize(x):
    mu = jnp.mean(x, axis=-1, keepdims=True)
    xc = x - mu
    var = jnp.mean(xc * xc, axis=-1, keepdims=True)
    return xc * lax.rsqrt(var + LN_EPS)


def _dot(a, b):
    return jnp.dot(a, b, preferred_element_type=F32)


def _resident(shape):
    zeros = (0,) * len(shape)
    return pl.BlockSpec(shape, lambda *_: zeros, pipeline_mode=pl.Buffered(1))


def _layer_resident(shape, layer):
    index = (layer,) + (0,) * len(shape)
    return pl.BlockSpec((None,) + tuple(shape), lambda *_: index, pipeline_mode=pl.Buffered(1))


def _layer_mod(layer, tiles_per_seq):
    return pl.BlockSpec((None, None, N_MOD, D_MODEL),
                        lambda i: (layer, i // tiles_per_seq, 0, 0))


def _params(*semantics):
    return pltpu.CompilerParams(dimension_semantics=semantics,
                                vmem_limit_bytes=VMEM_LIMIT_BYTES)


def _mod_kernel(c_ref, w_ref, b_ref, o_ref):
    c = c_ref[...]
    c_act = c * _sigmoid(c)
    o_ref[...] = jnp.dot(c_act, w_ref[...], preferred_element_type=F32,
                         precision=lax.Precision.HIGHEST) + b_ref[...]


def _modulation(c, w_ada, b_ada):
    depth = w_ada.shape[0]
    batch = c.shape[0]
    rows = -(-batch // SUBLANES) * SUBLANES
    c_pad = jnp.pad(c, ((0, rows - batch), (0, 0)))
    cols = N_MOD * D_MODEL
    out = pl.pallas_call(
        _mod_kernel,
        grid=(depth, N_MOD),
        in_specs=[
            pl.BlockSpec((rows, D_MODEL), lambda l, j: (0, 0)),
            pl.BlockSpec((None, D_MODEL, D_MODEL), lambda l, j: (l, 0, j)),
            pl.BlockSpec((None, 1, D_MODEL), lambda l, j: (l, 0, j)),
        ],
        out_specs=pl.BlockSpec((None, rows, D_MODEL), lambda l, j: (l, 0, j)),
        out_shape=jax.ShapeDtypeStruct((depth, rows, cols), F32),
        compiler_params=_params("arbitrary", "arbitrary"),
        name="adaln_mod",
    )(c_pad, w_ada, b_ada.reshape(depth, 1, cols))
    return out[:, :batch].reshape(depth, batch, N_MOD, D_MODEL)


def _inproj_kernel(x_ref, mod_ref, w_ref, qkv_ref, u_ref, g_ref, h_ref):
    shift = mod_ref[0:1, :]
    scale = mod_ref[1:2, :]
    h_ref[...] = (_normalize(x_ref[...]) * (1.0 + scale) + shift).astype(BF16)
    h = h_ref[...]
    w = SB_WIDTH
    qkv_ref[:, 0:w] = (_dot(h, w_ref[:, 0:w]) * (1.0 / math.sqrt(SB_HEAD_DIM))).astype(BF16)
    qkv_ref[:, w:2 * w] = _dot(h, w_ref[:, w:2 * w]).astype(BF16)
    qkv_ref[:, 2 * w:3 * w] = _dot(h, w_ref[:, 2 * w:3 * w]).astype(BF16)
    u_ref[...] = _dot(h, w_ref[:, 3 * w:3 * w + SSM_WIDTH])
    g0 = 3 * w + SSM_WIDTH
    for j in range(2 * D_MODEL // w):
        g_ref[:, j * w:(j + 1) * w] = _dot(h, w_ref[:, g0 + j * w:g0 + (j + 1) * w]).astype(BF16)


def _in_projection(x2d, mod, w_in_bf16, layer, seq):
    n = x2d.shape[0]
    tiles_per_seq = seq // ROW_TILE
    return pl.pallas_call(
        _inproj_kernel,
        grid=(n // ROW_TILE,),
        in_specs=[
            pl.BlockSpec((ROW_TILE, D_MODEL), lambda i: (i, 0)),
            _layer_mod(layer, tiles_per_seq),
            _layer_resident((D_MODEL, IN_COLS), layer),
        ],
        out_specs=[
            pl.BlockSpec((ROW_TILE, 3 * SB_WIDTH), lambda i: (i, 0)),
            pl.BlockSpec((ROW_TILE, SSM_WIDTH), lambda i: (i, 0)),
            pl.BlockSpec((ROW_TILE, 2 * D_MODEL), lambda i: (i, 0)),
        ],
        out_shape=[
            jax.ShapeDtypeStruct((n, 3 * SB_WIDTH), BF16),
            jax.ShapeDtypeStruct((n, SSM_WIDTH), F32),
            jax.ShapeDtypeStruct((n, 2 * D_MODEL), BF16),
        ],
        scratch_shapes=[pltpu.VMEM((ROW_TILE, D_MODEL), BF16)],
        compiler_params=_params("arbitrary"),
        name="in_proj",
    )(x2d, mod, w_in_bf16)


def _attn_kernel(q_ref, k_ref, v_ref, o_ref, acc_ref, carry_ref):
    t = ATT_TILE
    first_block = pl.program_id(2) * ATT_BLOCKS_PER_STEP
    row = lax.broadcasted_iota(jnp.int32, (t, t), 0)
    col = lax.broadcasted_iota(jnp.int32, (t, t), 1)
    suffix = jnp.where(row > col, 1.0, 0.0).astype(BF16)
    suffix2 = jnp.concatenate([suffix, suffix], axis=0)
    causal = col < row

    def visit(blocks):
        chains = []
        for qb, j, diagonal in blocks:
            start = pl.multiple_of(j * t, t)
            for hh in range(HEADS_PER_STEP):
                lanes = slice(hh * SB_HEAD_DIM, (hh + 1) * SB_HEAD_DIM)
                chains.append(dict(qb=qb, hh=hh, diagonal=diagonal, start=start, lanes=lanes))
        carries = {}

        def scores(c):
            kj = k_ref[pl.ds(c["start"], t), c["lanes"]]
            c["z"] = lax.dot_general(q_ref[c["qb"] * t:(c["qb"] + 1) * t, c["lanes"]], kj,
                                     (((1,), (1,)), ((), ())), preferred_element_type=F32)

        def log_terms(c):
            z = c.pop("z")
            log_beta = jnp.minimum(z, 0.0) - jnp.log(1.0 + jnp.exp(-jnp.abs(z)))
            lom = log_beta - z
            if c["diagonal"]:
                lom = jnp.where(causal, lom, 0.0)
            hi = lom.astype(BF16)
            lo = (lom - hi.astype(F32)).astype(BF16)
            c.update(log_beta=log_beta, first=lom[:, 0:1], split=jnp.concatenate([hi, lo], axis=1))

        def suffix_sums(c):
            c["after"] = _dot(c.pop("split"), suffix2)

        def accumulate(c):
            key = (c["qb"], c["hh"])
            if key not in carries:
                carries[key] = carry_ref[key]
            after = c.pop("after")
            w = jnp.exp(c.pop("log_beta") + after + carries[key])
            if c["diagonal"]:
                w = jnp.where(causal, w, 0.0)
            vj = v_ref[pl.ds(c["start"], t), c["lanes"]]
            acc_ref[key] += _dot(w.astype(BF16), vj)
            carries[key] = carries[key] + after[:, 0:1] + c.pop("first")

        stages = (scores, log_terms, suffix_sums, accumulate)
        for tick in range(len(chains) + len(stages) - 1):
            for s, stage in enumerate(stages):
                if 0 <= tick - s < len(chains):
                    stage(chains[tick - s])
        worst = {}
        for key, carry in carries.items():
            carry_ref[key] = carry
            worst[key[0]] = jnp.maximum(worst[key[0]], carry) if key[0] in worst else carry
        return tuple(jnp.max(worst[qb]) for qb in sorted(worst))

    acc_ref[...] = jnp.zeros_like(acc_ref)
    carry_ref[...] = jnp.zeros_like(carry_ref)

    diagonals = [(qb, first_block + qb, True) for qb in range(ATT_BLOCKS_PER_STEP)]
    previous = [(qb, first_block + qb - 1, False) for qb in range(ATT_BLOCKS_PER_STEP)]
    order = lambda blocks: sorted(blocks, key=lambda b: (b[0], not b[2]))
    worst = lax.cond(first_block > 0,
                     lambda: visit(order(diagonals + previous)),
                     lambda: visit(order(diagonals + previous[1:])))

    for qb in range(ATT_BLOCKS_PER_STEP):
        def more(state):
            j, worst = state
            return jnp.logical_and(j >= 0, worst > SB_EXIT_LOG)

        def step(state, qb=qb):
            j, _ = state
            return j - 1, visit([(qb, j, False)])[0]

        lax.while_loop(more, step, (first_block + qb - 2, worst[qb]))
        for hh in range(HEADS_PER_STEP):
            o_ref[qb * t:(qb + 1) * t, hh * SB_HEAD_DIM:(hh + 1) * SB_HEAD_DIM] = (
                acc_ref[qb, hh].astype(BF16))


def _attention(qkv, batch, seq):
    n = qkv.shape[0]
    t = ATT_TILE
    rows = ATT_BLOCKS_PER_STEP * t
    steps = seq // rows
    pairs = SB_HEADS // HEADS_PER_STEP
    return pl.pallas_call(
        _attn_kernel,
        grid=(batch, pairs, steps),
        in_specs=[
            pl.BlockSpec((rows, LANES), lambda b, p, i: (b * steps + i, p)),
            pl.BlockSpec((seq, LANES), lambda b, p, i: (b, pairs + p)),
            pl.BlockSpec((seq, LANES), lambda b, p, i: (b, 2 * pairs + p)),
        ],
        out_specs=pl.BlockSpec((rows, LANES), lambda b, p, i: (b * steps + i, p)),
        out_shape=jax.ShapeDtypeStruct((n, SB_WIDTH), BF16),
        scratch_shapes=[
            pltpu.VMEM((ATT_BLOCKS_PER_STEP, HEADS_PER_STEP, t, SB_HEAD_DIM), F32),
            pltpu.VMEM((ATT_BLOCKS_PER_STEP, HEADS_PER_STEP, t, 1), F32)],
        compiler_params=_params("arbitrary", "arbitrary", "arbitrary"),
        name="sb_attention",
    )(qkv, qkv, qkv)


def _gelu_tanh(x):
    c = math.sqrt(2.0 / math.pi)
    return 0.5 * x * (1.0 + jnp.tanh(c * (x + 0.044715 * (x * x * x))))


def _s5_kernel(u_ref, wb_ref, einv_re_ref, einv_im_ref, e_re_ref, e_im_ref,
               wc_ref, d_ref, wglu_ref, bglu_ref, o_ref,
               st_ref, bu_ref, xs_ref, cum_ref, hs_ref, y_ref):
    ell = SSM_CHUNK
    nb = SSM_BLOCK
    n_chunks = SSM_ROWS // ell

    @pl.when(pl.program_id(1) == 0)
    def _():
        st_ref[...] = jnp.zeros_like(st_ref)

    row = lax.broadcasted_iota(jnp.int32, (SCAN_ROWS, SCAN_ROWS), 0)
    col = lax.broadcasted_iota(jnp.int32, (SCAN_ROWS, SCAN_ROWS), 1)
    same_chunk = (row // ell) == (col // ell)
    prefix = jnp.where(jnp.logical_and(same_chunk, col <= row), 1.0, 0.0).astype(BF16)

    u = u_ref[...]
    u16 = u.astype(BF16)
    blocks = range(N_STATE // nb)
    states = [slice(sb * nb, (sb + 1) * nb) for sb in blocks]
    chans = [slice(sb * SSM_BLOCK_CH, (sb + 1) * SSM_BLOCK_CH) for sb in blocks]
    for sb in blocks:
        bu_ref[sb] = _dot(u16[:, chans[sb]], wb_ref[sb])
    for sb in blocks:
        v_re = einv_re_ref[:, states[sb]]
        v_im = einv_im_ref[:, states[sb]]
        for c in range(n_chunks):
            rows = slice(c * ell, (c + 1) * ell)
            b_re = bu_ref[sb, rows, :nb]
            b_im = bu_ref[sb, rows, nb:]
            xs_ref[sb, rows, :nb] = (b_re * v_re - b_im * v_im).astype(BF16)
            xs_ref[sb, rows, nb:] = (b_re * v_im + b_im * v_re).astype(BF16)
    for sb in blocks:
        for r0 in range(0, SSM_ROWS, SCAN_ROWS):
            slab = slice(r0, r0 + SCAN_ROWS)
            cum_ref[sb, slab, :] = _dot(prefix, xs_ref[sb, slab, :])
    for sb in blocks:
        e_re = e_re_ref[:, states[sb]]
        e_im = e_im_ref[:, states[sb]]
        st_re = st_ref[0:1, states[sb]]
        st_im = st_ref[1:2, states[sb]]
        for c in range(n_chunks):
            rows = slice(c * ell, (c + 1) * ell)
            a_re = cum_ref[sb, rows, :nb] + st_re
            a_im = cum_ref[sb, rows, nb:] + st_im
            h_re = e_re * a_re - e_im * a_im
            h_im = e_re * a_im + e_im * a_re
            st_re = h_re[ell - 1:ell, :]
            st_im = h_im[ell - 1:ell, :]
            hs_ref[sb, rows, :nb] = h_re.astype(BF16)
            hs_ref[sb, rows, nb:] = h_im.astype(BF16)
        st_ref[0:1, states[sb]] = st_re
        st_ref[1:2, states[sb]] = st_im
    for sb in blocks:
        y_ref[:, chans[sb]] = _dot(hs_ref[sb], wc_ref[sb])
    y = _gelu_tanh(y_ref[...] + d_ref[...] * u)
    gate = _dot(y.astype(BF16), wglu_ref[...]) + bglu_ref[...]
    o_ref[...] = (y * _sigmoid(gate)).astype(BF16)


def _s5_tables(a_re, a_im, log_dt, b_re, b_im, c_re, c_im):
    g, p, cg = SSM_GROUPS, SSM_STATE, SSM_GROUP
    gb = SSM_BLOCK // p
    nblk = g // gb
    dt = jnp.exp(log_dt)[:, None]
    ld_re, ld_im = a_re * dt, a_im * dt

    def cexp(re, im):
        mag = jnp.exp(re)
        return mag * jnp.cos(im), mag * jnp.sin(im)

    lb_re, lb_im = cexp(ld_re, ld_im)
    den = a_re * a_re + a_im * a_im
    f_re = ((lb_re - 1.0) * a_re + lb_im * a_im) / den
    f_im = (lb_im * a_re - (lb_re - 1.0) * a_im) / den
    bb_re = f_re[..., None] * b_re - f_im[..., None] * b_im
    bb_im = f_re[..., None] * b_im + f_im[..., None] * b_re
    eye = jnp.eye(gb, dtype=F32)

    def in_map(b):
        blocks = jnp.einsum('bgpc,gh->bgchp', b.reshape(nblk, gb, p, cg), eye)
        return blocks.reshape(nblk, gb * cg, gb * p)

    def out_map(c):
        blocks = jnp.einsum('bgcp,gh->bgphc', c.reshape(nblk, gb, cg, p), eye)
        return blocks.reshape(nblk, gb * p, gb * cg)

    wb = jnp.concatenate([in_map(bb_re), in_map(bb_im)], axis=2).astype(BF16)
    wc = jnp.concatenate([out_map(c_re), -out_map(c_im)], axis=1).astype(BF16)
    steps = jnp.arange(1, SSM_CHUNK + 1, dtype=F32)[:, None, None]
    e_re, e_im = cexp(ld_re[None] * steps, ld_im[None] * steps)
    v_re, v_im = cexp(-ld_re[None] * steps, -ld_im[None] * steps)
    flat = lambda a: a.reshape(SSM_CHUNK, g * p)
    return wb, flat(v_re), flat(v_im), flat(e_re), flat(e_im), wc


def _s5(u, tables, d_skip, w_glu_bf16, b_glu, layer, batch, seq):
    n = u.shape[0]
    wb, einv_re, einv_im, e_re, e_im, wc = tables
    steps = seq // SSM_ROWS
    nblk = N_STATE // SSM_BLOCK
    tab = (SSM_CHUNK, N_STATE)
    return pl.pallas_call(
        _s5_kernel,
        grid=(batch, steps),
        in_specs=[
            pl.BlockSpec((SSM_ROWS, SSM_WIDTH), lambda b, i: (b * steps + i, 0)),
            _resident(wb.shape),
            _resident(tab), _resident(tab), _resident(tab), _resident(tab),
            _resident(wc.shape),
            _layer_resident((1, SSM_WIDTH), layer),
            _layer_resident((SSM_WIDTH, SSM_WIDTH), layer),
            _layer_resident((1, SSM_WIDTH), layer),
        ],
        out_specs=pl.BlockSpec((SSM_ROWS, SSM_WIDTH), lambda b, i: (b * steps + i, 0)),
        out_shape=jax.ShapeDtypeStruct((n, SSM_WIDTH), BF16),
        scratch_shapes=[
            pltpu.VMEM((2, N_STATE), F32),
            pltpu.VMEM((nblk, SSM_ROWS, 2 * SSM_BLOCK), F32),
            pltpu.VMEM((nblk, SSM_ROWS, 2 * SSM_BLOCK), BF16),
            pltpu.VMEM((nblk, SSM_ROWS, 2 * SSM_BLOCK), F32),
            pltpu.VMEM((nblk, SSM_ROWS, 2 * SSM_BLOCK), BF16),
            pltpu.VMEM((SSM_ROWS, SSM_WIDTH), F32),
        ],
        compiler_params=_params("arbitrary", "arbitrary"),
        name="s5_scan",
    )(u, wb, einv_re, einv_im, e_re, e_im, wc,
      d_skip.reshape(-1, 1, SSM_WIDTH), w_glu_bf16, b_glu.reshape(-1, 1, SSM_WIDTH))


def _mixout_kernel(x_ref, att_ref, s5_ref, g_ref, mod_ref, wsb_ref, wssm_ref,
                   wout_ref, lng_ref, lnb_ref, o_ref, *, alpha):
    gate = mod_ref[2:3, :]
    slabs = [slice(r, r + MIX_SLAB) for r in range(0, ROW_TILE, MIX_SLAB)]
    branches = [(_dot(att_ref[rows, :], wsb_ref[...]), _dot(s5_ref[rows, :], wssm_ref[...]))
                for rows in slabs]
    merged = []
    for rows, (y_sb, y_ssm) in zip(slabs, branches):
        g_sb = g_ref[rows, :D_MODEL].astype(F32)
        g_ssm = g_ref[rows, D_MODEL:].astype(F32)
        merged.append((_sigmoid(g_sb) * y_sb + _sigmoid(g_ssm) * y_ssm).astype(BF16))
    ys = [_dot(m, wout_ref[...]) for m in merged]
    for rows, y in zip(slabs, ys):
        r = alpha * x_ref[rows, :] + (1.0 + gate) * y
        o_ref[rows, :] = _normalize(r) * lng_ref[...] + lnb_ref[...]


def _mix_out(x2d, att, s5, g, mod, w_sb_up, w_ssm_up, w_out, ln_g, ln_b, layer, seq, alpha):
    n = x2d.shape[0]
    tiles_per_seq = seq // ROW_TILE
    rows = lambda width: pl.BlockSpec((ROW_TILE, width), lambda i: (i, 0))
    return pl.pallas_call(
        functools.partial(_mixout_kernel, alpha=alpha),
        grid=(n // ROW_TILE,),
        in_specs=[
            rows(D_MODEL), rows(SB_WIDTH), rows(SSM_WIDTH), rows(2 * D_MODEL),
            _layer_mod(layer, tiles_per_seq),
            _layer_resident((SB_WIDTH, D_MODEL), layer),
            _layer_resident((SSM_WIDTH, D_MODEL), layer),
            _layer_resident((D_MODEL, D_MODEL), layer),
            _layer_resident((1, D_MODEL), layer),
            _layer_resident((1, D_MODEL), layer),
        ],
        out_specs=rows(D_MODEL),
        out_shape=jax.ShapeDtypeStruct((n, D_MODEL), F32),
        compiler_params=_params("arbitrary"),
        name="mix_out",
    )(x2d, att, s5, g, mod, w_sb_up, w_ssm_up, w_out,
      ln_g.reshape(-1, 1, D_MODEL), ln_b.reshape(-1, 1, D_MODEL))


def _ffn_kernel(x_ref, mod_ref, win_ref, wout_ref, lng_ref, lnb_ref, o_ref,
                h_ref, acc_ref, *, alpha):
    x = x_ref[...]
    shift = mod_ref[3:4, :]
    scale = mod_ref[4:5, :]
    gate_mod = mod_ref[5:6, :]
    h_ref[...] = (_normalize(x) * (1.0 + scale) + shift).astype(BF16)
    h = h_ref[...]
    for j in range(FFN_HIDDEN // FFN_CHUNK):
        cols = slice(j * FFN_CHUNK, (j + 1) * FFN_CHUNK)
        up_cols = slice(FFN_HIDDEN + j * FFN_CHUNK, FFN_HIDDEN + (j + 1) * FFN_CHUNK)
        gate = _dot(h, win_ref[:, cols])
        up = _dot(h, win_ref[:, up_cols])
        act = (gate * _sigmoid(gate) * up).astype(BF16)
        part = _dot(act, wout_ref[cols, :])
        if j == 0:
            acc_ref[...] = part
        else:
            acc_ref[...] += part
    r = alpha * x + (1.0 + gate_mod) * acc_ref[...]
    o_ref[...] = _normalize(r) * lng_ref[...] + lnb_ref[...]


def _ffn(x2d, mod, w_ffn_in, w_ffn_out, ln_g, ln_b, layer, seq, alpha):
    n = x2d.shape[0]
    tiles_per_seq = seq // ROW_TILE
    return pl.pallas_call(
        functools.partial(_ffn_kernel, alpha=alpha),
        grid=(n // ROW_TILE,),
        in_specs=[
            pl.BlockSpec((ROW_TILE, D_MODEL), lambda i: (i, 0)),
            _layer_mod(layer, tiles_per_seq),
            _layer_resident((D_MODEL, 2 * FFN_HIDDEN), layer),
            _layer_resident((FFN_HIDDEN, D_MODEL), layer),
            _layer_resident((1, D_MODEL), layer),
            _layer_resident((1, D_MODEL), layer),
        ],
        out_specs=pl.BlockSpec((ROW_TILE, D_MODEL), lambda i: (i, 0)),
        out_shape=jax.ShapeDtypeStruct((n, D_MODEL), F32),
        scratch_shapes=[pltpu.VMEM((ROW_TILE, D_MODEL), BF16),
                        pltpu.VMEM((ROW_TILE, D_MODEL), F32)],
        compiler_params=_params("arbitrary"),
        name="ffn",
    )(x2d, mod, w_ffn_in, w_ffn_out, ln_g.reshape(-1, 1, D_MODEL), ln_b.reshape(-1, 1, D_MODEL))


def kernel(x, c, w_ada, b_ada, w_in, w_sb_up, ssm_a_re, ssm_a_im, ssm_log_dt,
           ssm_b_re, ssm_b_im, ssm_c_re, ssm_c_im, ssm_d, w_glu, b_glu,
           w_ssm_up, w_out, ln1_g, ln1_b, w_ffn_in, w_ffn_out, ln2_g, ln2_b):
    batch, seq, d = x.shape
    depth = w_in.shape[0]
    assert d == D_MODEL and seq % max(ROW_TILE, ATT_TILE * ATT_BLOCKS_PER_STEP, SSM_ROWS) == 0
    alpha = (2 * depth) ** 0.25
    mod = _modulation(c, w_ada, b_ada)
    x2d = x.reshape(batch * seq, d)
    w_in, w_sb_up, w_ssm_up, w_out, w_glu, w_ffn_in, w_ffn_out = (
        w.astype(BF16) for w in (w_in, w_sb_up, w_ssm_up, w_out, w_glu, w_ffn_in, w_ffn_out))
    for l in range(depth):
        qkv, u, g = _in_projection(x2d, mod, w_in, l, seq)
        att = _attention(qkv, batch, seq)
        tables = _s5_tables(ssm_a_re[l], ssm_a_im[l], ssm_log_dt[l], ssm_b_re[l],
                            ssm_b_im[l], ssm_c_re[l], ssm_c_im[l])
        s5 = _s5(u, tables, ssm_d, w_glu, b_glu, l, batch, seq)
        x2d = _mix_out(x2d, att, s5, g, mod, w_sb_up, w_ssm_up, w_out, ln1_g, ln1_b,
                       l, seq, alpha)
        x2d = _ffn(x2d, mod, w_ffn_in, w_ffn_out, ln2_g, ln2_b, l, seq, alpha)
    return x2d.reshape(batch, seq, d)
```

```python
import functools
import math

import jax
import jax.numpy as jnp
from jax import lax
from jax.experimental import pallas as pl
from jax.experimental.pallas import tpu as pltpu

F32 = jnp.float32
BF16 = jnp.bfloat16

D_MODEL = 1024
SB_HEADS = 8
SB_HEAD_DIM = 64
SB_WIDTH = SB_HEADS * SB_HEAD_DIM
SSM_WIDTH = D_MODEL // 2
SSM_GROUP = 16
SSM_GROUPS = SSM_WIDTH // SSM_GROUP
SSM_STATE = 64
N_STATE = SSM_GROUPS * SSM_STATE
FFN_HIDDEN = 2816
IN_COLS = 3 * SB_WIDTH + SSM_WIDTH + 2 * D_MODEL
N_MOD = 6
LN_EPS = 1e-5

LANES = 128
SUBLANES = 8
VMEM_LIMIT_BYTES = 56 * 1024 * 1024

ROW_TILE = 512
MIX_SLAB = 256
ATT_TILE = 256
ATT_BLOCKS_PER_STEP = 4
SSM_CHUNK = 16
SSM_ROWS = 512
SSM_BLOCK = 512
SSM_BLOCK_CH = SSM_BLOCK // SSM_STATE * SSM_GROUP
SCAN_ROWS = 256
FFN_CHUNK = 256
HEADS_PER_STEP = LANES // SB_HEAD_DIM
SB_EXIT_LOG = -105.0


def _sigmoid(x):
    return 1.0 / (1.0 + jnp.exp(-x))


def _normalize(x):
    mu = jnp.mean(x, axis=-1, keepdims=True)
    xc = x - mu
    var = jnp.mean(xc * xc, axis=-1, keepdims=True)
    return xc * lax.rsqrt(var + LN_EPS)


def _dot(a, b):
    return jnp.dot(a, b, preferred_element_type=F32)


def _resident(shape):
    zeros = (0,) * len(shape)
    return pl.BlockSpec(shape, lambda *_: zeros, pipeline_mode=pl.Buffered(1))


def _layer_resident(shape, layer):
    index = (layer,) + (0,) * len(shape)
    return pl.BlockSpec((None,) + tuple(shape), lambda *_: index, pipeline_mode=pl.Buffered(1))


def _layer_mod(layer, tiles_per_seq):
    return pl.BlockSpec((None, None, N_MOD, D_MODEL),
                        lambda i: (layer, i // tiles_per_seq, 0, 0))


def _params(*semantics):
    return pltpu.CompilerParams(dimension_semantics=semantics,
                                vmem_limit_bytes=VMEM_LIMIT_BYTES)


def _mod_kernel(c_ref, w_ref, b_ref, o_ref):
    c = c_ref[...]
    c_act = c * _sigmoid(c)
    o_ref[...] = jnp.dot(c_act, w_ref[...], preferred_element_type=F32,
                         precision=lax.Precision.HIGHEST) + b_ref[...]


def _modulation(c, w_ada, b_ada):
    depth = w_ada.shape[0]
    batch = c.shape[0]
    rows = -(-batch // SUBLANES) * SUBLANES
    c_pad = jnp.pad(c, ((0, rows - batch), (0, 0)))
    cols = N_MOD * D_MODEL
    out = pl.pallas_call(
        _mod_kernel,
        grid=(depth, N_MOD),
        in_specs=[
            pl.BlockSpec((rows, D_MODEL), lambda l, j: (0, 0)),
            pl.BlockSpec((None, D_MODEL, D_MODEL), lambda l, j: (l, 0, j)),
            pl.BlockSpec((None, 1, D_MODEL), lambda l, j: (l, 0, j)),
        ],
        out_specs=pl.BlockSpec((None, rows, D_MODEL), lambda l, j: (l, 0, j)),
        out_shape=jax.ShapeDtypeStruct((depth, rows, cols), F32),
        compiler_params=_params("arbitrary", "arbitrary"),
        name="adaln_mod",
    )(c_pad, w_ada, b_ada.reshape(depth, 1, cols))
    return out[:, :batch].reshape(depth, batch, N_MOD, D_MODEL)


def _inproj_kernel(x_ref, mod_ref, w_ref, qkv_ref, u_ref, g_ref, h_ref):
    shift = mod_ref[0:1, :]
    scale = mod_ref[1:2, :]
    h_ref[...] = (_normalize(x_ref[...]) * (1.0 + scale) + shift).astype(BF16)
    h = h_ref[...]
    w = SB_WIDTH
    qkv_ref[:, 0:w] = (_dot(h, w_ref[:, 0:w]) * (1.0 / math.sqrt(SB_HEAD_DIM))).astype(BF16)
    qkv_ref[:, w:2 * w] = _dot(h, w_ref[:, w:2 * w]).astype(BF16)
    qkv_ref[:, 2 * w:3 * w] = _dot(h, w_ref[:, 2 * w:3 * w]).astype(BF16)
    u_ref[...] = _dot(h, w_ref[:, 3 * w:3 * w + SSM_WIDTH])
    g0 = 3 * w + SSM_WIDTH
    for j in range(2 * D_MODEL // w):
        g_ref[:, j * w:(j + 1) * w] = _dot(h, w_ref[:, g0 + j * w:g0 + (j + 1) * w]).astype(BF16)


def _in_projection(x2d, mod, w_in_bf16, layer, seq):
    n = x2d.shape[0]
    tiles_per_seq = seq // ROW_TILE
    return pl.pallas_call(
        _inproj_kernel,
        grid=(n // ROW_TILE,),
        in_specs=[
            pl.BlockSpec((ROW_TILE, D_MODEL), lambda i: (i, 0)),
            _layer_mod(layer, tiles_per_seq),
            _layer_resident((D_MODEL, IN_COLS), layer),
        ],
        out_specs=[
            pl.BlockSpec((ROW_TILE, 3 * SB_WIDTH), lambda i: (i, 0)),
            pl.BlockSpec((ROW_TILE, SSM_WIDTH), lambda i: (i, 0)),
            pl.BlockSpec((ROW_TILE, 2 * D_MODEL), lambda i: (i, 0)),
        ],
        out_shape=[
            jax.ShapeDtypeStruct((n, 3 * SB_WIDTH), BF16),
            jax.ShapeDtypeStruct((n, SSM_WIDTH), F32),
            jax.ShapeDtypeStruct((n, 2 * D_MODEL), BF16),
        ],
        scratch_shapes=[pltpu.VMEM((ROW_TILE, D_MODEL), BF16)],
        compiler_params=_params("arbitrary"),
        name="in_proj",
    )(x2d, mod, w_in_bf16)


def _attn_kernel(q_ref, k_ref, v_ref, o_ref, acc_ref, carry_ref):
    t = ATT_TILE
    first_block = pl.program_id(2) * ATT_BLOCKS_PER_STEP
    row = lax.broadcasted_iota(jnp.int32, (t, t), 0)
    col = lax.broadcasted_iota(jnp.int32, (t, t), 1)
    suffix = jnp.where(row > col, 1.0, 0.0).astype(BF16)
    suffix2 = jnp.concatenate([suffix, suffix], axis=0)
    causal = col < row

    def visit(blocks):
        chains = []
        for qb, j, diagonal in blocks:
            start = pl.multiple_of(j * t, t)
            for hh in range(HEADS_PER_STEP):
                lanes = slice(hh * SB_HEAD_DIM, (hh + 1) * SB_HEAD_DIM)
                kj = k_ref[pl.ds(start, t), lanes]
                z = lax.dot_general(q_ref[qb * t:(qb + 1) * t, lanes], kj,
                                    (((1,), (1,)), ((), ())), preferred_element_type=F32)
                chains.append(dict(qb=qb, hh=hh, diagonal=diagonal, start=start, lanes=lanes, z=z))
        for c in chains:
            z = c.pop("z")
            log_beta = jnp.minimum(z, 0.0) - jnp.log(1.0 + jnp.exp(-jnp.abs(z)))
            lom = log_beta - z
            if c["diagonal"]:
                lom = jnp.where(causal, lom, 0.0)
            hi = lom.astype(BF16)
            lo = (lom - hi.astype(F32)).astype(BF16)
            c.update(log_beta=log_beta, first=lom[:, 0:1], split=jnp.concatenate([hi, lo], axis=1))
        for c in chains:
            c["after"] = _dot(c.pop("split"), suffix2)
        carries = {}
        for c in chains:
            key = (c["qb"], c["hh"])
            if key not in carries:
                carries[key] = carry_ref[key]
            after = c["after"]
            w = jnp.exp(c["log_beta"] + after + carries[key])
            if c["diagonal"]:
                w = jnp.where(causal, w, 0.0)
            vj = v_ref[pl.ds(c["start"], t), c["lanes"]]
            acc_ref[key] += _dot(w.astype(BF16), vj)
            carries[key] = carries[key] + after[:, 0:1] + c["first"]
        worst = {}
        for key, carry in carries.items():
            carry_ref[key] = carry
            top = jnp.max(carry)
            worst[key[0]] = jnp.maximum(worst[key[0]], top) if key[0] in worst else top
        return tuple(worst[qb] for qb in sorted(worst))

    acc_ref[...] = jnp.zeros_like(acc_ref)
    carry_ref[...] = jnp.zeros_like(carry_ref)

    diagonals = [(qb, first_block + qb, True) for qb in range(ATT_BLOCKS_PER_STEP)]
    previous = [(qb, first_block + qb - 1, False) for qb in range(ATT_BLOCKS_PER_STEP)]
    order = lambda blocks: sorted(blocks, key=lambda b: (b[0], not b[2]))
    worst = lax.cond(first_block > 0,
                     lambda: visit(order(diagonals + previous)),
                     lambda: visit(order(diagonals + previous[1:])))

    for qb in range(ATT_BLOCKS_PER_STEP):
        def more(state):
            j, worst = state
            return jnp.logical_and(j >= 0, worst > SB_EXIT_LOG)

        def step(state, qb=qb):
            j, _ = state
            return j - 1, visit([(qb, j, False)])[0]

        lax.while_loop(more, step, (first_block + qb - 2, worst[qb]))
        for hh in range(HEADS_PER_STEP):
            o_ref[qb * t:(qb + 1) * t, hh * SB_HEAD_DIM:(hh + 1) * SB_HEAD_DIM] = (
                acc_ref[qb, hh].astype(BF16))


def _attention(qkv, batch, seq):
    n = qkv.shape[0]
    t = ATT_TILE
    rows = ATT_BLOCKS_PER_STEP * t
    steps = seq // rows
    pairs = SB_HEADS // HEADS_PER_STEP
    return pl.pallas_call(
        _attn_kernel,
        grid=(batch, pairs, steps),
        in_specs=[
            pl.BlockSpec((rows, LANES), lambda b, p, i: (b * steps + i, p)),
            pl.BlockSpec((seq, LANES), lambda b, p, i: (b, pairs + p)),
            pl.BlockSpec((seq, LANES), lambda b, p, i: (b, 2 * pairs + p)),
        ],
        out_specs=pl.BlockSpec((rows, LANES), lambda b, p, i: (b * steps + i, p)),
        out_shape=jax.ShapeDtypeStruct((n, SB_WIDTH), BF16),
        scratch_shapes=[
            pltpu.VMEM((ATT_BLOCKS_PER_STEP, HEADS_PER_STEP, t, SB_HEAD_DIM), F32),
            pltpu.VMEM((ATT_BLOCKS_PER_STEP, HEADS_PER_STEP, t, 1), F32)],
        compiler_params=_params("arbitrary", "arbitrary", "arbitrary"),
        name="sb_attention",
    )(qkv, qkv, qkv)


def _gelu_tanh(x):
    c = math.sqrt(2.0 / math.pi)
    return 0.5 * x * (1.0 + jnp.tanh(c * (x + 0.044715 * (x * x * x))))


def _s5_kernel(u_ref, wb_ref, einv_re_ref, einv_im_ref, e_re_ref, e_im_ref,
               wc_ref, d_ref, wglu_ref, bglu_ref, o_ref,
               st_ref, bu_ref, xs_ref, cum_ref, hs_ref, y_ref):
    ell = SSM_CHUNK
    nb = SSM_BLOCK
    n_chunks = SSM_ROWS // ell

    @pl.when(pl.program_id(1) == 0)
    def _():
        st_ref[...] = jnp.zeros_like(st_ref)

    row = lax.broadcasted_iota(jnp.int32, (SCAN_ROWS, SCAN_ROWS), 0)
    col = lax.broadcasted_iota(jnp.int32, (SCAN_ROWS, SCAN_ROWS), 1)
    same_chunk = (row // ell) == (col // ell)
    prefix = jnp.where(jnp.logical_and(same_chunk, col <= row), 1.0, 0.0).astype(BF16)

    u = u_ref[...]
    u16 = u.astype(BF16)
    blocks = range(N_STATE // nb)
    states = [slice(sb * nb, (sb + 1) * nb) for sb in blocks]
    chans = [slice(sb * SSM_BLOCK_CH, (sb + 1) * SSM_BLOCK_CH) for sb in blocks]
    for sb in blocks:
        bu_ref[sb] = _dot(u16[:, chans[sb]], wb_ref[sb])
    for sb in blocks:
        v_re = einv_re_ref[:, states[sb]]
        v_im = einv_im_ref[:, states[sb]]
        for c in range(n_chunks):
            rows = slice(c * ell, (c + 1) * ell)
            b_re = bu_ref[sb, rows, :nb]
            b_im = bu_ref[sb, rows, nb:]
            xs_ref[sb, rows, :nb] = (b_re * v_re - b_im * v_im).astype(BF16)
            xs_ref[sb, rows, nb:] = (b_re * v_im + b_im * v_re).astype(BF16)
    for sb in blocks:
        for r0 in range(0, SSM_ROWS, SCAN_ROWS):
            slab = slice(r0, r0 + SCAN_ROWS)
            cum_ref[sb, slab, :] = _dot(prefix, xs_ref[sb, slab, :])
    for sb in blocks:
        e_re = e_re_ref[:, states[sb]]
        e_im = e_im_ref[:, states[sb]]
        st_re = st_ref[0:1, states[sb]]
        st_im = st_ref[1:2, states[sb]]
        for c in range(n_chunks):
            rows = slice(c * ell, (c + 1) * ell)
            a_re = cum_ref[sb, rows, :nb] + st_re
            a_im = cum_ref[sb, rows, nb:] + st_im
            h_re = e_re * a_re - e_im * a_im
            h_im = e_re * a_im + e_im * a_re
            st_re = h_re[ell - 1:ell, :]
            st_im = h_im[ell - 1:ell, :]
            hs_ref[sb, rows, :nb] = h_re.astype(BF16)
            hs_ref[sb, rows, nb:] = h_im.astype(BF16)
        st_ref[0:1, states[sb]] = st_re
        st_ref[1:2, states[sb]] = st_im
    for sb in blocks:
        y_ref[:, chans[sb]] = _dot(hs_ref[sb], wc_ref[sb])
    y = _gelu_tanh(y_ref[...] + d_ref[...] * u)
    gate = _dot(y.astype(BF16), wglu_ref[...]) + bglu_ref[...]
    o_ref[...] = (y * _sigmoid(gate)).astype(BF16)


def _s5_tables(a_re, a_im, log_dt, b_re, b_im, c_re, c_im):
    g, p, cg = SSM_GROUPS, SSM_STATE, SSM_GROUP
    gb = SSM_BLOCK // p
    nblk = g // gb
    dt = jnp.exp(log_dt)[:, None]
    ld_re, ld_im = a_re * dt, a_im * dt

    def cexp(re, im):
        mag = jnp.exp(re)
        return mag * jnp.cos(im), mag * jnp.sin(im)

    lb_re, lb_im = cexp(ld_re, ld_im)
    den = a_re * a_re + a_im * a_im
    f_re = ((lb_re - 1.0) * a_re + lb_im * a_im) / den
    f_im = (lb_im * a_re - (lb_re - 1.0) * a_im) / den
    bb_re = f_re[..., None] * b_re - f_im[..., None] * b_im
    bb_im = f_re[..., None] * b_im + f_im[..., None] * b_re
    eye = jnp.eye(gb, dtype=F32)

    def in_map(b):
        blocks = jnp.einsum('bgpc,gh->bgchp', b.reshape(nblk, gb, p, cg), eye)
        return blocks.reshape(nblk, gb * cg, gb * p)

    def out_map(c):
        blocks = jnp.einsum('bgcp,gh->bgphc', c.reshape(nblk, gb, cg, p), eye)
        return blocks.reshape(nblk, gb * p, gb * cg)

    wb = jnp.concatenate([in_map(bb_re), in_map(bb_im)], axis=2).astype(BF16)
    wc = jnp.concatenate([out_map(c_re), -out_map(c_im)], axis=1).astype(BF16)
    steps = jnp.arange(1, SSM_CHUNK + 1, dtype=F32)[:, None, None]
    e_re, e_im = cexp(ld_re[None] * steps, ld_im[None] * steps)
    v_re, v_im = cexp(-ld_re[None] * steps, -ld_im[None] * steps)
    flat = lambda a: a.reshape(SSM_CHUNK, g * p)
    return wb, flat(v_re), flat(v_im), flat(e_re), flat(e_im), wc


def _s5(u, tables, d_skip, w_glu_bf16, b_glu, layer, batch, seq):
    n = u.shape[0]
    wb, einv_re, einv_im, e_re, e_im, wc = tables
    steps = seq // SSM_ROWS
    nblk = N_STATE // SSM_BLOCK
    tab = (SSM_CHUNK, N_STATE)
    return pl.pallas_call(
        _s5_kernel,
        grid=(batch, steps),
        in_specs=[
            pl.BlockSpec((SSM_ROWS, SSM_WIDTH), lambda b, i: (b * steps + i, 0)),
            _resident(wb.shape),
            _resident(tab), _resident(tab), _resident(tab), _resident(tab),
            _resident(wc.shape),
            _layer_resident((1, SSM_WIDTH), layer),
            _layer_resident((SSM_WIDTH, SSM_WIDTH), layer),
            _layer_resident((1, SSM_WIDTH), layer),
        ],
        out_specs=pl.BlockSpec((SSM_ROWS, SSM_WIDTH), lambda b, i: (b * steps + i, 0)),
        out_shape=jax.ShapeDtypeStruct((n, SSM_WIDTH), BF16),
        scratch_shapes=[
            pltpu.VMEM((2, N_STATE), F32),
            pltpu.VMEM((nblk, SSM_ROWS, 2 * SSM_BLOCK), F32),
            pltpu.VMEM((nblk, SSM_ROWS, 2 * SSM_BLOCK), BF16),
            pltpu.VMEM((nblk, SSM_ROWS, 2 * SSM_BLOCK), F32),
            pltpu.VMEM((nblk, SSM_ROWS, 2 * SSM_BLOCK), BF16),
            pltpu.VMEM((SSM_ROWS, SSM_WIDTH), F32),
        ],
        compiler_params=_params("arbitrary", "arbitrary"),
        name="s5_scan",
    )(u, wb, einv_re, einv_im, e_re, e_im, wc,
      d_skip.reshape(-1, 1, SSM_WIDTH), w_glu_bf16, b_glu.reshape(-1, 1, SSM_WIDTH))


def _mixout_kernel(x_ref, att_ref, s5_ref, g_ref, mod_ref, wsb_ref, wssm_ref,
                   wout_ref, lng_ref, lnb_ref, o_ref, *, alpha):
    gate = mod_ref[2:3, :]
    slabs = [slice(r, r + MIX_SLAB) for r in range(0, ROW_TILE, MIX_SLAB)]
    branches = [(_dot(att_ref[rows, :], wsb_ref[...]), _dot(s5_ref[rows, :], wssm_ref[...]))
                for rows in slabs]
    merged = []
    for rows, (y_sb, y_ssm) in zip(slabs, branches):
        g_sb = g_ref[rows, :D_MODEL].astype(F32)
        g_ssm = g_ref[rows, D_MODEL:].astype(F32)
        merged.append((_sigmoid(g_sb) * y_sb + _sigmoid(g_ssm) * y_ssm).astype(BF16))
    ys = [_dot(m, wout_ref[...]) for m in merged]
    for rows, y in zip(slabs, ys):
        r = alpha * x_ref[rows, :] + (1.0 + gate) * y
        o_ref[rows, :] = _normalize(r) * lng_ref[...] + lnb_ref[...]


def _mix_out(x2d, att, s5, g, mod, w_sb_up, w_ssm_up, w_out, ln_g, ln_b, layer, seq, alpha):
    n = x2d.shape[0]
    tiles_per_seq = seq // ROW_TILE
    rows = lambda width: pl.BlockSpec((ROW_TILE, width), lambda i: (i, 0))
    return pl.pallas_call(
        functools.partial(_mixout_kernel, alpha=alpha),
        grid=(n // ROW_TILE,),
        in_specs=[
            rows(D_MODEL), rows(SB_WIDTH), rows(SSM_WIDTH), rows(2 * D_MODEL),
            _layer_mod(layer, tiles_per_seq),
            _layer_resident((SB_WIDTH, D_MODEL), layer),
            _layer_resident((SSM_WIDTH, D_MODEL), layer),
            _layer_resident((D_MODEL, D_MODEL), layer),
            _layer_resident((1, D_MODEL), layer),
            _layer_resident((1, D_MODEL), layer),
        ],
        out_specs=rows(D_MODEL),
        out_shape=jax.ShapeDtypeStruct((n, D_MODEL), F32),
        compiler_params=_params("arbitrary"),
        name="mix_out",
    )(x2d, att, s5, g, mod, w_sb_up, w_ssm_up, w_out,
      ln_g.reshape(-1, 1, D_MODEL), ln_b.reshape(-1, 1, D_MODEL))


def _ffn_kernel(x_ref, mod_ref, win_ref, wout_ref, lng_ref, lnb_ref, o_ref,
                h_ref, acc_ref, *, alpha):
    x = x_ref[...]
    shift = mod_ref[3:4, :]
    scale = mod_ref[4:5, :]
    gate_mod = mod_ref[5:6, :]
    h_ref[...] = (_normalize(x) * (1.0 + scale) + shift).astype(BF16)
    h = h_ref[...]
    for j in range(FFN_HIDDEN // FFN_CHUNK):
        cols = slice(j * FFN_CHUNK, (j + 1) * FFN_CHUNK)
        up_cols = slice(FFN_HIDDEN + j * FFN_CHUNK, FFN_HIDDEN + (j + 1) * FFN_CHUNK)
        gate = _dot(h, win_ref[:, cols])
        up = _dot(h, win_ref[:, up_cols])
        act = (gate * _sigmoid(gate) * up).astype(BF16)
        part = _dot(act, wout_ref[cols, :])
        if j == 0:
            acc_ref[...] = part
        else:
            acc_ref[...] += part
    r = alpha * x + (1.0 + gate_mod) * acc_ref[...]
    o_ref[...] = _normalize(r) * lng_ref[...] + lnb_ref[...]


def _ffn(x2d, mod, w_ffn_in, w_ffn_out, ln_g, ln_b, layer, seq, alpha):
    n = x2d.shape[0]
    tiles_per_seq = seq // ROW_TILE
    return pl.pallas_call(
        functools.partial(_ffn_kernel, alpha=alpha),
        grid=(n // ROW_TILE,),
        in_specs=[
            pl.BlockSpec((ROW_TILE, D_MODEL), lambda i: (i, 0)),
            _layer_mod(layer, tiles_per_seq),
            _layer_resident((D_MODEL, 2 * FFN_HIDDEN), layer),
            _layer_resident((FFN_HIDDEN, D_MODEL), layer),
            _layer_resident((1, D_MODEL), layer),
            _layer_resident((1, D_MODEL), layer),
        ],
        out_specs=pl.BlockSpec((ROW_TILE, D_MODEL), lambda i: (i, 0)),
        out_shape=jax.ShapeDtypeStruct((n, D_MODEL), F32),
        scratch_shapes=[pltpu.VMEM((ROW_TILE, D_MODEL), BF16),
                        pltpu.VMEM((ROW_TILE, D_MODEL), F32)],
        compiler_params=_params("arbitrary"),
        name="ffn",
    )(x2d, mod, w_ffn_in, w_ffn_out, ln_g.reshape(-1, 1, D_MODEL), ln_b.reshape(-1, 1, D_MODEL))


def kernel(x, c, w_ada, b_ada, w_in, w_sb_up, ssm_a_re, ssm_a_im, ssm_log_dt,
           ssm_b_re, ssm_b_im, ssm_c_re, ssm_c_im, ssm_d, w_glu, b_glu,
           w_ssm_up, w_out, ln1_g, ln1_b, w_ffn_in, w_ffn_out, ln2_g, ln2_b):
    batch, seq, d = x.shape
    depth = w_in.shape[0]
    assert d == D_MODEL and seq % max(ROW_TILE, ATT_TILE * ATT_BLOCKS_PER_STEP, SSM_ROWS) == 0
    alpha = (2 * depth) ** 0.25
    mod = _modulation(c, w_ada, b_ada)
    x2d = x.reshape(batch * seq, d)
    w_in, w_sb_up, w_ssm_up, w_out, w_glu, w_ffn_in, w_ffn_out = (
        w.astype(BF16) for w in (w_in, w_sb_up, w_ssm_up, w_out, w_glu, w_ffn_in, w_ffn_out))
    for l in range(depth):
        qkv, u, g = _in_projection(x2d, mod, w_in, l, seq)
        att = _attention(qkv, batch, seq)
        tables = _s5_tables(ssm_a_re[l], ssm_a_im[l], ssm_log_dt[l], ssm_b_re[l],
                            ssm_b_im[l], ssm_c_re[l], ssm_c_im[l])
        s5 = _s5(u, tables, ssm_d, w_glu, b_glu, l, batch, seq)
        x2d = _mix_out(x2d, att, s5, g, mod, w_sb_up, w_ssm_up, w_out, ln1_g, ln1_b,
                       l, seq, alpha)
        x2d = _ffn(x2d, mod, w_ffn_in, w_ffn_out, ln2_g, ln2_b, l, seq, alpha)
    return x2d.reshape(batch, seq, d)
```

```python
import functools
import math

import jax
import jax.numpy as jnp
from jax import lax
from jax.experimental import pallas as pl
from jax.experimental.pallas import tpu as pltpu

F32 = jnp.float32
BF16 = jnp.bfloat16

D_MODEL = 1024
SB_HEADS = 8
SB_HEAD_DIM = 64
SB_WIDTH = SB_HEADS * SB_HEAD_DIM
SSM_WIDTH = D_MODEL // 2
SSM_GROUP = 16
SSM_GROUPS = SSM_WIDTH // SSM_GROUP
SSM_STATE = 64
N_STATE = SSM_GROUPS * SSM_STATE
FFN_HIDDEN = 2816
IN_COLS = 3 * SB_WIDTH + SSM_WIDTH + 2 * D_MODEL
N_MOD = 6
LN_EPS = 1e-5

LANES = 128
SUBLANES = 8
VMEM_LIMIT_BYTES = 56 * 1024 * 1024

ROW_TILE = 512
MIX_SLAB = 256
ATT_TILE = 256
ATT_BLOCKS_PER_STEP = 8
SSM_CHUNK = 16
SSM_ROWS = 512
SSM_BLOCK = 512
SSM_BLOCK_CH = SSM_BLOCK // SSM_STATE * SSM_GROUP
SCAN_ROWS = 256
FFN_CHUNK = 256
HEADS_PER_STEP = LANES // SB_HEAD_DIM
SB_EXIT_LOG = -105.0


def _sigmoid(x):
    return 1.0 / (1.0 + jnp.exp(-x))


def _normalize(x):
    mu = jnp.mean(x, axis=-1, keepdims=True)
    xc = x - mu
    var = jnp.mean(xc * xc, axis=-1, keepdims=True)
    return xc * lax.rsqrt(var + LN_EPS)


def _dot(a, b):
    return jnp.dot(a, b, preferred_element_type=F32)


def _resident(shape):
    zeros = (0,) * len(shape)
    return pl.BlockSpec(shape, lambda *_: zeros, pipeline_mode=pl.Buffered(1))


def _layer_resident(shape, layer):
    index = (layer,) + (0,) * len(shape)
    return pl.BlockSpec((None,) + tuple(shape), lambda *_: index, pipeline_mode=pl.Buffered(1))


def _layer_mod(layer, tiles_per_seq):
    return pl.BlockSpec((None, None, N_MOD, D_MODEL),
                        lambda i: (layer, i // tiles_per_seq, 0, 0))


def _params(*semantics):
    return pltpu.CompilerParams(dimension_semantics=semantics,
                                vmem_limit_bytes=VMEM_LIMIT_BYTES)


def _mod_kernel(c_ref, w_ref, b_ref, o_ref):
    c = c_ref[...]
    c_act = c * _sigmoid(c)
    o_ref[...] = jnp.dot(c_act, w_ref[...], preferred_element_type=F32,
                         precision=lax.Precision.HIGHEST) + b_ref[...]


def _modulation(c, w_ada, b_ada):
    depth = w_ada.shape[0]
    batch = c.shape[0]
    rows = -(-batch // SUBLANES) * SUBLANES
    c_pad = jnp.pad(c, ((0, rows - batch), (0, 0)))
    cols = N_MOD * D_MODEL
    out = pl.pallas_call(
        _mod_kernel,
        grid=(depth, N_MOD),
        in_specs=[
            pl.BlockSpec((rows, D_MODEL), lambda l, j: (0, 0)),
            pl.BlockSpec((None, D_MODEL, D_MODEL), lambda l, j: (l, 0, j)),
            pl.BlockSpec((None, 1, D_MODEL), lambda l, j: (l, 0, j)),
        ],
        out_specs=pl.BlockSpec((None, rows, D_MODEL), lambda l, j: (l, 0, j)),
        out_shape=jax.ShapeDtypeStruct((depth, rows, cols), F32),
        compiler_params=_params("arbitrary", "arbitrary"),
        name="adaln_mod",
    )(c_pad, w_ada, b_ada.reshape(depth, 1, cols))
    return out[:, :batch].reshape(depth, batch, N_MOD, D_MODEL)


def _inproj_kernel(x_ref, mod_ref, w_ref, qkv_ref, u_ref, g_ref, h_ref):
    shift = mod_ref[0:1, :]
    scale = mod_ref[1:2, :]
    h_ref[...] = (_normalize(x_ref[...]) * (1.0 + scale) + shift).astype(BF16)
    h = h_ref[...]
    w = SB_WIDTH
    qkv_ref[:, 0:w] = (_dot(h, w_ref[:, 0:w]) * (1.0 / math.sqrt(SB_HEAD_DIM))).astype(BF16)
    qkv_ref[:, w:2 * w] = _dot(h, w_ref[:, w:2 * w]).astype(BF16)
    qkv_ref[:, 2 * w:3 * w] = _dot(h, w_ref[:, 2 * w:3 * w]).astype(BF16)
    u_ref[...] = _dot(h, w_ref[:, 3 * w:3 * w + SSM_WIDTH])
    g0 = 3 * w + SSM_WIDTH
    for j in range(2 * D_MODEL // w):
        g_ref[:, j * w:(j + 1) * w] = _dot(h, w_ref[:, g0 + j * w:g0 + (j + 1) * w]).astype(BF16)


def _in_projection(x2d, mod, w_in_bf16, layer, seq):
    n = x2d.shape[0]
    tiles_per_seq = seq // ROW_TILE
    return pl.pallas_call(
        _inproj_kernel,
        grid=(n // ROW_TILE,),
        in_specs=[
            pl.BlockSpec((ROW_TILE, D_MODEL), lambda i: (i, 0)),
            _layer_mod(layer, tiles_per_seq),
            _layer_resident((D_MODEL, IN_COLS), layer),
        ],
        out_specs=[
            pl.BlockSpec((ROW_TILE, 3 * SB_WIDTH), lambda i: (i, 0)),
            pl.BlockSpec((ROW_TILE, SSM_WIDTH), lambda i: (i, 0)),
            pl.BlockSpec((ROW_TILE, 2 * D_MODEL), lambda i: (i, 0)),
        ],
        out_shape=[
            jax.ShapeDtypeStruct((n, 3 * SB_WIDTH), BF16),
            jax.ShapeDtypeStruct((n, SSM_WIDTH), F32),
            jax.ShapeDtypeStruct((n, 2 * D_MODEL), BF16),
        ],
        scratch_shapes=[pltpu.VMEM((ROW_TILE, D_MODEL), BF16)],
        compiler_params=_params("arbitrary"),
        name="in_proj",
    )(x2d, mod, w_in_bf16)


def _attn_kernel(q_ref, k_ref, v_ref, o_ref, acc_ref, carry_ref):
    t = ATT_TILE
    first_block = pl.program_id(2) * ATT_BLOCKS_PER_STEP
    row = lax.broadcasted_iota(jnp.int32, (t, t), 0)
    col = lax.broadcasted_iota(jnp.int32, (t, t), 1)
    suffix = jnp.where(row > col, 1.0, 0.0).astype(BF16)
    suffix2 = jnp.concatenate([suffix, suffix], axis=0)
    causal = col < row

    def visit(blocks):
        chains = []
        for qb, j, diagonal in blocks:
            start = pl.multiple_of(j * t, t)
            for hh in range(HEADS_PER_STEP):
                lanes = slice(hh * SB_HEAD_DIM, (hh + 1) * SB_HEAD_DIM)
                kj = k_ref[pl.ds(start, t), lanes]
                z = lax.dot_general(q_ref[qb * t:(qb + 1) * t, lanes], kj,
                                    (((1,), (1,)), ((), ())), preferred_element_type=F32)
                chains.append(dict(qb=qb, hh=hh, diagonal=diagonal, start=start, lanes=lanes, z=z))
        for c in chains:
            z = c.pop("z")
            log_beta = jnp.minimum(z, 0.0) - jnp.log(1.0 + jnp.exp(-jnp.abs(z)))
            lom = log_beta - z
            if c["diagonal"]:
                lom = jnp.where(causal, lom, 0.0)
            hi = lom.astype(BF16)
            lo = (lom - hi.astype(F32)).astype(BF16)
            c.update(log_beta=log_beta, first=lom[:, 0:1], split=jnp.concatenate([hi, lo], axis=1))
        for c in chains:
            c["after"] = _dot(c.pop("split"), suffix2)
        carries = {}
        for c in chains:
            key = (c["qb"], c["hh"])
            if key not in carries:
                carries[key] = carry_ref[key]
            after = c["after"]
            w = jnp.exp(c["log_beta"] + after + carries[key])
            if c["diagonal"]:
                w = jnp.where(causal, w, 0.0)
            vj = v_ref[pl.ds(c["start"], t), c["lanes"]]
            acc_ref[key] += _dot(w.astype(BF16), vj)
            carries[key] = carries[key] + after[:, 0:1] + c["first"]
        worst = {}
        for key, carry in carries.items():
            carry_ref[key] = carry
            top = jnp.max(carry)
            worst[key[0]] = jnp.maximum(worst[key[0]], top) if key[0] in worst else top
        return tuple(worst[qb] for qb in sorted(worst))

    acc_ref[...] = jnp.zeros_like(acc_ref)
    carry_ref[...] = jnp.zeros_like(carry_ref)

    diagonals = [(qb, first_block + qb, True) for qb in range(ATT_BLOCKS_PER_STEP)]
    previous = [(qb, first_block + qb - 1, False) for qb in range(ATT_BLOCKS_PER_STEP)]
    order = lambda blocks: sorted(blocks, key=lambda b: (b[0], not b[2]))
    worst = lax.cond(first_block > 0,
                     lambda: visit(order(diagonals + previous)),
                     lambda: visit(order(diagonals + previous[1:])))

    for qb in range(ATT_BLOCKS_PER_STEP):
        def more(state):
            j, worst = state
            return jnp.logical_and(j >= 0, worst > SB_EXIT_LOG)

        def step(state, qb=qb):
            j, _ = state
            return j - 1, visit([(qb, j, False)])[0]

        lax.while_loop(more, step, (first_block + qb - 2, worst[qb]))
        for hh in range(HEADS_PER_STEP):
            o_ref[qb * t:(qb + 1) * t, hh * SB_HEAD_DIM:(hh + 1) * SB_HEAD_DIM] = (
                acc_ref[qb, hh].astype(BF16))


def _attention(qkv, batch, seq):
    n = qkv.shape[0]
    t = ATT_TILE
    rows = ATT_BLOCKS_PER_STEP * t
    steps = seq // rows
    pairs = SB_HEADS // HEADS_PER_STEP
    return pl.pallas_call(
        _attn_kernel,
        grid=(batch, pairs, steps),
        in_specs=[
            pl.BlockSpec((rows, LANES), lambda b, p, i: (b * steps + i, p)),
            pl.BlockSpec((seq, LANES), lambda b, p, i: (b, pairs + p)),
            pl.BlockSpec((seq, LANES), lambda b, p, i: (b, 2 * pairs + p)),
        ],
        out_specs=pl.BlockSpec((rows, LANES), lambda b, p, i: (b * steps + i, p)),
        out_shape=jax.ShapeDtypeStruct((n, SB_WIDTH), BF16),
        scratch_shapes=[
            pltpu.VMEM((ATT_BLOCKS_PER_STEP, HEADS_PER_STEP, t, SB_HEAD_DIM), F32),
            pltpu.VMEM((ATT_BLOCKS_PER_STEP, HEADS_PER_STEP, t, 1), F32)],
        compiler_params=_params("arbitrary", "arbitrary", "arbitrary"),
        name="sb_attention",
    )(qkv, qkv, qkv)


def _gelu_tanh(x):
    c = math.sqrt(2.0 / math.pi)
    return 0.5 * x * (1.0 + jnp.tanh(c * (x + 0.044715 * (x * x * x))))


def _s5_kernel(u_ref, wb_ref, einv_re_ref, einv_im_ref, e_re_ref, e_im_ref,
               wc_ref, d_ref, wglu_ref, bglu_ref, o_ref,
               st_ref, bu_ref, xs_ref, cum_ref, hs_ref, y_ref):
    ell = SSM_CHUNK
    nb = SSM_BLOCK
    n_chunks = SSM_ROWS // ell

    @pl.when(pl.program_id(1) == 0)
    def _():
        st_ref[...] = jnp.zeros_like(st_ref)

    row = lax.broadcasted_iota(jnp.int32, (SCAN_ROWS, SCAN_ROWS), 0)
    col = lax.broadcasted_iota(jnp.int32, (SCAN_ROWS, SCAN_ROWS), 1)
    same_chunk = (row // ell) == (col // ell)
    prefix = jnp.where(jnp.logical_and(same_chunk, col <= row), 1.0, 0.0).astype(BF16)

    u = u_ref[...]
    u16 = u.astype(BF16)
    blocks = range(N_STATE // nb)
    states = [slice(sb * nb, (sb + 1) * nb) for sb in blocks]
    chans = [slice(sb * SSM_BLOCK_CH, (sb + 1) * SSM_BLOCK_CH) for sb in blocks]
    for sb in blocks:
        bu_ref[sb] = _dot(u16[:, chans[sb]], wb_ref[sb])
    for sb in blocks:
        v_re = einv_re_ref[:, states[sb]]
        v_im = einv_im_ref[:, states[sb]]
        for c in range(n_chunks):
            rows = slice(c * ell, (c + 1) * ell)
            b_re = bu_ref[sb, rows, :nb]
            b_im = bu_ref[sb, rows, nb:]
            xs_ref[sb, rows, :nb] = (b_re * v_re - b_im * v_im).astype(BF16)
            xs_ref[sb, rows, nb:] = (b_re * v_im + b_im * v_re).astype(BF16)
    for sb in blocks:
        for r0 in range(0, SSM_ROWS, SCAN_ROWS):
            slab = slice(r0, r0 + SCAN_ROWS)
            cum_ref[sb, slab, :] = _dot(prefix, xs_ref[sb, slab, :])
    for sb in blocks:
        e_re = e_re_ref[:, states[sb]]
        e_im = e_im_ref[:, states[sb]]
        st_re = st_ref[0:1, states[sb]]
        st_im = st_ref[1:2, states[sb]]
        for c in range(n_chunks):
            rows = slice(c * ell, (c + 1) * ell)
            a_re = cum_ref[sb, rows, :nb] + st_re
            a_im = cum_ref[sb, rows, nb:] + st_im
            h_re = e_re * a_re - e_im * a_im
            h_im = e_re * a_im + e_im * a_re
            st_re = h_re[ell - 1:ell, :]
            st_im = h_im[ell - 1:ell, :]
            hs_ref[sb, rows, :nb] = h_re.astype(BF16)
            hs_ref[sb, rows, nb:] = h_im.astype(BF16)
        st_ref[0:1, states[sb]] = st_re
        st_ref[1:2, states[sb]] = st_im
    for sb in blocks:
        y_ref[:, chans[sb]] = _dot(hs_ref[sb], wc_ref[sb])
    y = _gelu_tanh(y_ref[...] + d_ref[...] * u)
    gate = _dot(y.astype(BF16), wglu_ref[...]) + bglu_ref[...]
    o_ref[...] = (y * _sigmoid(gate)).astype(BF16)


def _s5_tables(a_re, a_im, log_dt, b_re, b_im, c_re, c_im):
    g, p, cg = SSM_GROUPS, SSM_STATE, SSM_GROUP
    gb = SSM_BLOCK // p
    nblk = g // gb
    dt = jnp.exp(log_dt)[:, None]
    ld_re, ld_im = a_re * dt, a_im * dt

    def cexp(re, im):
        mag = jnp.exp(re)
        return mag * jnp.cos(im), mag * jnp.sin(im)

    lb_re, lb_im = cexp(ld_re, ld_im)
    den = a_re * a_re + a_im * a_im
    f_re = ((lb_re - 1.0) * a_re + lb_im * a_im) / den
    f_im = (lb_im * a_re - (lb_re - 1.0) * a_im) / den
    bb_re = f_re[..., None] * b_re - f_im[..., None] * b_im
    bb_im = f_re[..., None] * b_im + f_im[..., None] * b_re
    eye = jnp.eye(gb, dtype=F32)

    def in_map(b):
        blocks = jnp.einsum('bgpc,gh->bgchp', b.reshape(nblk, gb, p, cg), eye)
        return blocks.reshape(nblk, gb * cg, gb * p)

    def out_map(c):
        blocks = jnp.einsum('bgcp,gh->bgphc', c.reshape(nblk, gb, cg, p), eye)
        return blocks.reshape(nblk, gb * p, gb * cg)

    wb = jnp.concatenate([in_map(bb_re), in_map(bb_im)], axis=2).astype(BF16)
    wc = jnp.concatenate([out_map(c_re), -out_map(c_im)], axis=1).astype(BF16)
    steps = jnp.arange(1, SSM_CHUNK + 1, dtype=F32)[:, None, None]
    e_re, e_im = cexp(ld_re[None] * steps, ld_im[None] * steps)
    v_re, v_im = cexp(-ld_re[None] * steps, -ld_im[None] * steps)
    flat = lambda a: a.reshape(SSM_CHUNK, g * p)
    return wb, flat(v_re), flat(v_im), flat(e_re), flat(e_im), wc


def _s5(u, tables, d_skip, w_glu_bf16, b_glu, layer, batch, seq):
    n = u.shape[0]
    wb, einv_re, einv_im, e_re, e_im, wc = tables
    steps = seq // SSM_ROWS
    nblk = N_STATE // SSM_BLOCK
    tab = (SSM_CHUNK, N_STATE)
    return pl.pallas_call(
        _s5_kernel,
        grid=(batch, steps),
        in_specs=[
            pl.BlockSpec((SSM_ROWS, SSM_WIDTH), lambda b, i: (b * steps + i, 0)),
            _resident(wb.shape),
            _resident(tab), _resident(tab), _resident(tab), _resident(tab),
            _resident(wc.shape),
            _layer_resident((1, SSM_WIDTH), layer),
            _layer_resident((SSM_WIDTH, SSM_WIDTH), layer),
            _layer_resident((1, SSM_WIDTH), layer),
        ],
        out_specs=pl.BlockSpec((SSM_ROWS, SSM_WIDTH), lambda b, i: (b * steps + i, 0)),
        out_shape=jax.ShapeDtypeStruct((n, SSM_WIDTH), BF16),
        scratch_shapes=[
            pltpu.VMEM((2, N_STATE), F32),
            pltpu.VMEM((nblk, SSM_ROWS, 2 * SSM_BLOCK), F32),
            pltpu.VMEM((nblk, SSM_ROWS, 2 * SSM_BLOCK), BF16),
            pltpu.VMEM((nblk, SSM_ROWS, 2 * SSM_BLOCK), F32),
            pltpu.VMEM((nblk, SSM_ROWS, 2 * SSM_BLOCK), BF16),
            pltpu.VMEM((SSM_ROWS, SSM_WIDTH), F32),
        ],
        compiler_params=_params("arbitrary", "arbitrary"),
        name="s5_scan",
    )(u, wb, einv_re, einv_im, e_re, e_im, wc,
      d_skip.reshape(-1, 1, SSM_WIDTH), w_glu_bf16, b_glu.reshape(-1, 1, SSM_WIDTH))


def _mixout_kernel(x_ref, att_ref, s5_ref, g_ref, mod_ref, wsb_ref, wssm_ref,
                   wout_ref, lng_ref, lnb_ref, o_ref, *, alpha):
    gate = mod_ref[2:3, :]
    slabs = [slice(r, r + MIX_SLAB) for r in range(0, ROW_TILE, MIX_SLAB)]
    branches = [(_dot(att_ref[rows, :], wsb_ref[...]), _dot(s5_ref[rows, :], wssm_ref[...]))
                for rows in slabs]
    merged = []
    for rows, (y_sb, y_ssm) in zip(slabs, branches):
        g_sb = g_ref[rows, :D_MODEL].astype(F32)
        g_ssm = g_ref[rows, D_MODEL:].astype(F32)
        merged.append((_sigmoid(g_sb) * y_sb + _sigmoid(g_ssm) * y_ssm).astype(BF16))
    ys = [_dot(m, wout_ref[...]) for m in merged]
    for rows, y in zip(slabs, ys):
        r = alpha * x_ref[rows, :] + (1.0 + gate) * y
        o_ref[rows, :] = _normalize(r) * lng_ref[...] + lnb_ref[...]


def _mix_out(x2d, att, s5, g, mod, w_sb_up, w_ssm_up, w_out, ln_g, ln_b, layer, seq, alpha):
    n = x2d.shape[0]
    tiles_per_seq = seq // ROW_TILE
    rows = lambda width: pl.BlockSpec((ROW_TILE, width), lambda i: (i, 0))
    return pl.pallas_call(
        functools.partial(_mixout_kernel, alpha=alpha),
        grid=(n // ROW_TILE,),
        in_specs=[
            rows(D_MODEL), rows(SB_WIDTH), rows(SSM_WIDTH), rows(2 * D_MODEL),
            _layer_mod(layer, tiles_per_seq),
            _layer_resident((SB_WIDTH, D_MODEL), layer),
            _layer_resident((SSM_WIDTH, D_MODEL), layer),
            _layer_resident((D_MODEL, D_MODEL), layer),
            _layer_resident((1, D_MODEL), layer),
            _layer_resident((1, D_MODEL), layer),
        ],
        out_specs=rows(D_MODEL),
        out_shape=jax.ShapeDtypeStruct((n, D_MODEL), F32),
        compiler_params=_params("arbitrary"),
        name="mix_out",
    )(x2d, att, s5, g, mod, w_sb_up, w_ssm_up, w_out,
      ln_g.reshape(-1, 1, D_MODEL), ln_b.reshape(-1, 1, D_MODEL))


def _ffn_kernel(x_ref, mod_ref, win_ref, wout_ref, lng_ref, lnb_ref, o_ref,
                h_ref, acc_ref, *, alpha):
    x = x_ref[...]
    shift = mod_ref[3:4, :]
    scale = mod_ref[4:5, :]
    gate_mod = mod_ref[5:6, :]
    h_ref[...] = (_normalize(x) * (1.0 + scale) + shift).astype(BF16)
    h = h_ref[...]
    for j in range(FFN_HIDDEN // FFN_CHUNK):
        cols = slice(j * FFN_CHUNK, (j + 1) * FFN_CHUNK)
        up_cols = slice(FFN_HIDDEN + j * FFN_CHUNK, FFN_HIDDEN + (j + 1) * FFN_CHUNK)
        gate = _dot(h, win_ref[:, cols])
        up = _dot(h, win_ref[:, up_cols])
        act = (gate * _sigmoid(gate) * up).astype(BF16)
        part = _dot(act, wout_ref[cols, :])
        if j == 0:
            acc_ref[...] = part
        else:
            acc_ref[...] += part
    r = alpha * x + (1.0 + gate_mod) * acc_ref[...]
    o_ref[...] = _normalize(r) * lng_ref[...] + lnb_ref[...]


def _ffn(x2d, mod, w_ffn_in, w_ffn_out, ln_g, ln_b, layer, seq, alpha):
    n = x2d.shape[0]
    tiles_per_seq = seq // ROW_TILE
    return pl.pallas_call(
        functools.partial(_ffn_kernel, alpha=alpha),
        grid=(n // ROW_TILE,),
        in_specs=[
            pl.BlockSpec((ROW_TILE, D_MODEL), lambda i: (i, 0)),
            _layer_mod(layer, tiles_per_seq),
            _layer_resident((D_MODEL, 2 * FFN_HIDDEN), layer),
            _layer_resident((FFN_HIDDEN, D_MODEL), layer),
            _layer_resident((1, D_MODEL), layer),
            _layer_resident((1, D_MODEL), layer),
        ],
        out_specs=pl.BlockSpec((ROW_TILE, D_MODEL), lambda i: (i, 0)),
        out_shape=jax.ShapeDtypeStruct((n, D_MODEL), F32),
        scratch_shapes=[pltpu.VMEM((ROW_TILE, D_MODEL), BF16),
                        pltpu.VMEM((ROW_TILE, D_MODEL), F32)],
        compiler_params=_params("arbitrary"),
        name="ffn",
    )(x2d, mod, w_ffn_in, w_ffn_out, ln_g.reshape(-1, 1, D_MODEL), ln_b.reshape(-1, 1, D_MODEL))


def kernel(x, c, w_ada, b_ada, w_in, w_sb_up, ssm_a_re, ssm_a_im, ssm_log_dt,
           ssm_b_re, ssm_b_im, ssm_c_re, ssm_c_im, ssm_d, w_glu, b_glu,
           w_ssm_up, w_out, ln1_g, ln1_b, w_ffn_in, w_ffn_out, ln2_g, ln2_b):
    batch, seq, d = x.shape
    depth = w_in.shape[0]
    assert d == D_MODEL and seq % max(ROW_TILE, ATT_TILE * ATT_BLOCKS_PER_STEP, SSM_ROWS) == 0
    alpha = (2 * depth) ** 0.25
    mod = _modulation(c, w_ada, b_ada)
    x2d = x.reshape(batch * seq, d)
    w_in, w_sb_up, w_ssm_up, w_out, w_glu, w_ffn_in, w_ffn_out = (
        w.astype(BF16) for w in (w_in, w_sb_up, w_ssm_up, w_out, w_glu, w_ffn_in, w_ffn_out))
    for l in range(depth):
        qkv, u, g = _in_projection(x2d, mod, w_in, l, seq)
        att = _attention(qkv, batch, seq)
        tables = _s5_tables(ssm_a_re[l], ssm_a_im[l], ssm_log_dt[l], ssm_b_re[l],
                            ssm_b_im[l], ssm_c_re[l], ssm_c_im[l])
        s5 = _s5(u, tables, ssm_d, w_glu, b_glu, l, batch, seq)
        x2d = _mix_out(x2d, att, s5, g, mod, w_sb_up, w_ssm_up, w_out, ln1_g, ln1_b,
                       l, seq, alpha)
        x2d = _ffn(x2d, mod, w_ffn_in, w_ffn_out, ln2_g, ln2_b, l, seq, alpha)
    return x2d.reshape(batch, seq, d)
```

```python
import functools
import math

import jax
import jax.numpy as jnp
from jax import lax
from jax.experimental import pallas as pl
from jax.experimental.pallas import tpu as pltpu

F32 = jnp.float32
BF16 = jnp.bfloat16

D_MODEL = 1024
SB_HEADS = 8
SB_HEAD_DIM = 64
SB_WIDTH = SB_HEADS * SB_HEAD_DIM
SSM_WIDTH = D_MODEL // 2
SSM_GROUP = 16
SSM_GROUPS = SSM_WIDTH // SSM_GROUP
SSM_STATE = 64
N_STATE = SSM_GROUPS * SSM_STATE
FFN_HIDDEN = 2816
IN_COLS = 3 * SB_WIDTH + SSM_WIDTH + 2 * D_MODEL
N_MOD = 6
LN_EPS = 1e-5

LANES = 128
SUBLANES = 8
VMEM_LIMIT_BYTES = 56 * 1024 * 1024

ROW_TILE = 512
MIX_SLAB = 256
ATT_TILE = 256
ATT_BLOCKS_PER_STEP = 8
SSM_CHUNK = 16
SSM_ROWS = 512
SSM_BLOCK = 256
SSM_BLOCK_CH = SSM_BLOCK // SSM_STATE * SSM_GROUP
SCAN_ROWS = 256
FFN_CHUNK = 256
HEADS_PER_STEP = LANES // SB_HEAD_DIM
SB_EXIT_LOG = -105.0


def _sigmoid(x):
    return 1.0 / (1.0 + jnp.exp(-x))


def _normalize(x):
    mu = jnp.mean(x, axis=-1, keepdims=True)
    xc = x - mu
    var = jnp.mean(xc * xc, axis=-1, keepdims=True)
    return xc * lax.rsqrt(var + LN_EPS)


def _dot(a, b):
    return jnp.dot(a, b, preferred_element_type=F32)


def _resident(shape):
    zeros = (0,) * len(shape)
    return pl.BlockSpec(shape, lambda *_: zeros, pipeline_mode=pl.Buffered(1))


def _layer_resident(shape, layer):
    index = (layer,) + (0,) * len(shape)
    return pl.BlockSpec((None,) + tuple(shape), lambda *_: index, pipeline_mode=pl.Buffered(1))


def _layer_mod(layer, tiles_per_seq):
    return pl.BlockSpec((None, None, N_MOD, D_MODEL),
                        lambda i: (layer, i // tiles_per_seq, 0, 0))


def _params(*semantics):
    return pltpu.CompilerParams(dimension_semantics=semantics,
                                vmem_limit_bytes=VMEM_LIMIT_BYTES)


def _mod_kernel(c_ref, w_ref, b_ref, o_ref):
    c = c_ref[...]
    c_act = c * _sigmoid(c)
    o_ref[...] = jnp.dot(c_act, w_ref[...], preferred_element_type=F32,
                         precision=lax.Precision.HIGHEST) + b_ref[...]


def _modulation(c, w_ada, b_ada):
    depth = w_ada.shape[0]
    batch = c.shape[0]
    rows = -(-batch // SUBLANES) * SUBLANES
    c_pad = jnp.pad(c, ((0, rows - batch), (0, 0)))
    cols = N_MOD * D_MODEL
    out = pl.pallas_call(
        _mod_kernel,
        grid=(depth, N_MOD),
        in_specs=[
            pl.BlockSpec((rows, D_MODEL), lambda l, j: (0, 0)),
            pl.BlockSpec((None, D_MODEL, D_MODEL), lambda l, j: (l, 0, j)),
            pl.BlockSpec((None, 1, D_MODEL), lambda l, j: (l, 0, j)),
        ],
        out_specs=pl.BlockSpec((None, rows, D_MODEL), lambda l, j: (l, 0, j)),
        out_shape=jax.ShapeDtypeStruct((depth, rows, cols), F32),
        compiler_params=_params("arbitrary", "arbitrary"),
        name="adaln_mod",
    )(c_pad, w_ada, b_ada.reshape(depth, 1, cols))
    return out[:, :batch].reshape(depth, batch, N_MOD, D_MODEL)


def _inproj_kernel(x_ref, mod_ref, w_ref, qkv_ref, u_ref, g_ref, h_ref):
    shift = mod_ref[0:1, :]
    scale = mod_ref[1:2, :]
    h_ref[...] = (_normalize(x_ref[...]) * (1.0 + scale) + shift).astype(BF16)
    h = h_ref[...]
    w = SB_WIDTH
    qkv_ref[:, 0:w] = (_dot(h, w_ref[:, 0:w]) * (1.0 / math.sqrt(SB_HEAD_DIM))).astype(BF16)
    qkv_ref[:, w:2 * w] = _dot(h, w_ref[:, w:2 * w]).astype(BF16)
    qkv_ref[:, 2 * w:3 * w] = _dot(h, w_ref[:, 2 * w:3 * w]).astype(BF16)
    u_ref[...] = _dot(h, w_ref[:, 3 * w:3 * w + SSM_WIDTH])
    g0 = 3 * w + SSM_WIDTH
    for j in range(2 * D_MODEL // w):
        g_ref[:, j * w:(j + 1) * w] = _dot(h, w_ref[:, g0 + j * w:g0 + (j + 1) * w]).astype(BF16)


def _in_projection(x2d, mod, w_in_bf16, layer, seq):
    n = x2d.shape[0]
    tiles_per_seq = seq // ROW_TILE
    return pl.pallas_call(
        _inproj_kernel,
        grid=(n // ROW_TILE,),
        in_specs=[
            pl.BlockSpec((ROW_TILE, D_MODEL), lambda i: (i, 0)),
            _layer_mod(layer, tiles_per_seq),
            _layer_resident((D_MODEL, IN_COLS), layer),
        ],
        out_specs=[
            pl.BlockSpec((ROW_TILE, 3 * SB_WIDTH), lambda i: (i, 0)),
            pl.BlockSpec((ROW_TILE, SSM_WIDTH), lambda i: (i, 0)),
            pl.BlockSpec((ROW_TILE, 2 * D_MODEL), lambda i: (i, 0)),
        ],
        out_shape=[
            jax.ShapeDtypeStruct((n, 3 * SB_WIDTH), BF16),
            jax.ShapeDtypeStruct((n, SSM_WIDTH), F32),
            jax.ShapeDtypeStruct((n, 2 * D_MODEL), BF16),
        ],
        scratch_shapes=[pltpu.VMEM((ROW_TILE, D_MODEL), BF16)],
        compiler_params=_params("arbitrary"),
        name="in_proj",
    )(x2d, mod, w_in_bf16)


def _attn_kernel(q_ref, k_ref, v_ref, o_ref, acc_ref, carry_ref):
    t = ATT_TILE
    first_block = pl.program_id(2) * ATT_BLOCKS_PER_STEP
    row = lax.broadcasted_iota(jnp.int32, (t, t), 0)
    col = lax.broadcasted_iota(jnp.int32, (t, t), 1)
    suffix = jnp.where(row > col, 1.0, 0.0).astype(BF16)
    suffix2 = jnp.concatenate([suffix, suffix], axis=0)
    causal = col < row

    def visit(blocks):
        chains = []
        for qb, j, diagonal in blocks:
            start = pl.multiple_of(j * t, t)
            for hh in range(HEADS_PER_STEP):
                lanes = slice(hh * SB_HEAD_DIM, (hh + 1) * SB_HEAD_DIM)
                kj = k_ref[pl.ds(start, t), lanes]
                z = lax.dot_general(q_ref[qb * t:(qb + 1) * t, lanes], kj,
                                    (((1,), (1,)), ((), ())), preferred_element_type=F32)
                chains.append(dict(qb=qb, hh=hh, diagonal=diagonal, start=start, lanes=lanes, z=z))
        for c in chains:
            z = c.pop("z")
            log_beta = jnp.minimum(z, 0.0) - jnp.log(1.0 + jnp.exp(-jnp.abs(z)))
            lom = log_beta - z
            if c["diagonal"]:
                lom = jnp.where(causal, lom, 0.0)
            hi = lom.astype(BF16)
            lo = (lom - hi.astype(F32)).astype(BF16)
            c.update(log_beta=log_beta, first=lom[:, 0:1], split=jnp.concatenate([hi, lo], axis=1))
        for c in chains:
            c["after"] = _dot(c.pop("split"), suffix2)
        carries = {}
        for c in chains:
            key = (c["qb"], c["hh"])
            if key not in carries:
                carries[key] = carry_ref[key]
            after = c["after"]
            w = jnp.exp(c["log_beta"] + after + carries[key])
            if c["diagonal"]:
                w = jnp.where(causal, w, 0.0)
            vj = v_ref[pl.ds(c["start"], t), c["lanes"]]
            acc_ref[key] += _dot(w.astype(BF16), vj)
            carries[key] = carries[key] + after[:, 0:1] + c["first"]
        worst = {}
        for key, carry in carries.items():
            carry_ref[key] = carry
            top = jnp.max(carry)
            worst[key[0]] = jnp.maximum(worst[key[0]], top) if key[0] in worst else top
        return tuple(worst[qb] for qb in sorted(worst))

    acc_ref[...] = jnp.zeros_like(acc_ref)
    carry_ref[...] = jnp.zeros_like(carry_ref)

    diagonals = [(qb, first_block + qb, True) for qb in range(ATT_BLOCKS_PER_STEP)]
    previous = [(qb, first_block + qb - 1, False) for qb in range(ATT_BLOCKS_PER_STEP)]
    order = lambda blocks: sorted(blocks, key=lambda b: (b[0], not b[2]))
    worst = lax.cond(first_block > 0,
                     lambda: visit(order(diagonals + previous)),
                     lambda: visit(order(diagonals + previous[1:])))

    for qb in range(ATT_BLOCKS_PER_STEP):
        def more(state):
            j, worst = state
            return jnp.logical_and(j >= 0, worst > SB_EXIT_LOG)

        def step(state, qb=qb):
            j, _ = state
            return j - 1, visit([(qb, j, False)])[0]

        lax.while_loop(more, step, (first_block + qb - 2, worst[qb]))
        for hh in range(HEADS_PER_STEP):
            o_ref[qb * t:(qb + 1) * t, hh * SB_HEAD_DIM:(hh + 1) * SB_HEAD_DIM] = (
                acc_ref[qb, hh].astype(BF16))


def _attention(qkv, batch, seq):
    n = qkv.shape[0]
    t = ATT_TILE
    rows = ATT_BLOCKS_PER_STEP * t
    steps = seq // rows
    pairs = SB_HEADS // HEADS_PER_STEP
    return pl.pallas_call(
        _attn_kernel,
        grid=(batch, pairs, steps),
        in_specs=[
            pl.BlockSpec((rows, LANES), lambda b, p, i: (b * steps + i, p)),
            pl.BlockSpec((seq, LANES), lambda b, p, i: (b, pairs + p)),
            pl.BlockSpec((seq, LANES), lambda b, p, i: (b, 2 * pairs + p)),
        ],
        out_specs=pl.BlockSpec((rows, LANES), lambda b, p, i: (b * steps + i, p)),
        out_shape=jax.ShapeDtypeStruct((n, SB_WIDTH), BF16),
        scratch_shapes=[
            pltpu.VMEM((ATT_BLOCKS_PER_STEP, HEADS_PER_STEP, t, SB_HEAD_DIM), F32),
            pltpu.VMEM((ATT_BLOCKS_PER_STEP, HEADS_PER_STEP, t, 1), F32)],
        compiler_params=_params("arbitrary", "arbitrary", "arbitrary"),
        name="sb_attention",
    )(qkv, qkv, qkv)


def _gelu_tanh(x):
    c = math.sqrt(2.0 / math.pi)
    return 0.5 * x * (1.0 + jnp.tanh(c * (x + 0.044715 * (x * x * x))))


def _s5_kernel(u_ref, wb_ref, einv_re_ref, einv_im_ref, e_re_ref, e_im_ref,
               wc_ref, d_ref, wglu_ref, bglu_ref, o_ref,
               st_ref, bu_ref, xs_ref, cum_ref, hs_ref, y_ref):
    ell = SSM_CHUNK
    nb = SSM_BLOCK
    n_chunks = SSM_ROWS // ell

    @pl.when(pl.program_id(1) == 0)
    def _():
        st_ref[...] = jnp.zeros_like(st_ref)

    row = lax.broadcasted_iota(jnp.int32, (SCAN_ROWS, SCAN_ROWS), 0)
    col = lax.broadcasted_iota(jnp.int32, (SCAN_ROWS, SCAN_ROWS), 1)
    same_chunk = (row // ell) == (col // ell)
    prefix = jnp.where(jnp.logical_and(same_chunk, col <= row), 1.0, 0.0).astype(BF16)

    u = u_ref[...]
    u16 = u.astype(BF16)
    blocks = range(N_STATE // nb)
    states = [slice(sb * nb, (sb + 1) * nb) for sb in blocks]
    chans = [slice(sb * SSM_BLOCK_CH, (sb + 1) * SSM_BLOCK_CH) for sb in blocks]
    for sb in blocks:
        bu_ref[sb] = _dot(u16[:, chans[sb]], wb_ref[sb])
    for sb in blocks:
        v_re = einv_re_ref[:, states[sb]]
        v_im = einv_im_ref[:, states[sb]]
        for c in range(n_chunks):
            rows = slice(c * ell, (c + 1) * ell)
            b_re = bu_ref[sb, rows, :nb]
            b_im = bu_ref[sb, rows, nb:]
            xs_ref[sb, rows, :nb] = (b_re * v_re - b_im * v_im).astype(BF16)
            xs_ref[sb, rows, nb:] = (b_re * v_im + b_im * v_re).astype(BF16)
    for sb in blocks:
        for r0 in range(0, SSM_ROWS, SCAN_ROWS):
            slab = slice(r0, r0 + SCAN_ROWS)
            cum_ref[sb, slab, :] = _dot(prefix, xs_ref[sb, slab, :])
    for sb in blocks:
        e_re = e_re_ref[:, states[sb]]
        e_im = e_im_ref[:, states[sb]]
        st_re = st_ref[0:1, states[sb]]
        st_im = st_ref[1:2, states[sb]]
        for c in range(n_chunks):
            rows = slice(c * ell, (c + 1) * ell)
            a_re = cum_ref[sb, rows, :nb] + st_re
            a_im = cum_ref[sb, rows, nb:] + st_im
            h_re = e_re * a_re - e_im * a_im
            h_im = e_re * a_im + e_im * a_re
            st_re = h_re[ell - 1:ell, :]
            st_im = h_im[ell - 1:ell, :]
            hs_ref[sb, rows, :nb] = h_re.astype(BF16)
            hs_ref[sb, rows, nb:] = h_im.astype(BF16)
        st_ref[0:1, states[sb]] = st_re
        st_ref[1:2, states[sb]] = st_im
    for sb in blocks:
        y_ref[:, chans[sb]] = _dot(hs_ref[sb], wc_ref[sb])
    y = _gelu_tanh(y_ref[...] + d_ref[...] * u)
    gate = _dot(y.astype(BF16), wglu_ref[...]) + bglu_ref[...]
    o_ref[...] = (y * _sigmoid(gate)).astype(BF16)


def _s5_tables(a_re, a_im, log_dt, b_re, b_im, c_re, c_im):
    g, p, cg = SSM_GROUPS, SSM_STATE, SSM_GROUP
    gb = SSM_BLOCK // p
    nblk = g // gb
    dt = jnp.exp(log_dt)[:, None]
    ld_re, ld_im = a_re * dt, a_im * dt

    def cexp(re, im):
        mag = jnp.exp(re)
        return mag * jnp.cos(im), mag * jnp.sin(im)

    lb_re, lb_im = cexp(ld_re, ld_im)
    den = a_re * a_re + a_im * a_im
    f_re = ((lb_re - 1.0) * a_re + lb_im * a_im) / den
    f_im = (lb_im * a_re - (lb_re - 1.0) * a_im) / den
    bb_re = f_re[..., None] * b_re - f_im[..., None] * b_im
    bb_im = f_re[..., None] * b_im + f_im[..., None] * b_re
    eye = jnp.eye(gb, dtype=F32)

    def in_map(b):
        blocks = jnp.einsum('bgpc,gh->bgchp', b.reshape(nblk, gb, p, cg), eye)
        return blocks.reshape(nblk, gb * cg, gb * p)

    def out_map(c):
        blocks = jnp.einsum('bgcp,gh->bgphc', c.reshape(nblk, gb, cg, p), eye)
        return blocks.reshape(nblk, gb * p, gb * cg)

    wb = jnp.concatenate([in_map(bb_re), in_map(bb_im)], axis=2).astype(BF16)
    wc = jnp.concatenate([out_map(c_re), -out_map(c_im)], axis=1).astype(BF16)
    steps = jnp.arange(1, SSM_CHUNK + 1, dtype=F32)[:, None, None]
    e_re, e_im = cexp(ld_re[None] * steps, ld_im[None] * steps)
    v_re, v_im = cexp(-ld_re[None] * steps, -ld_im[None] * steps)
    flat = lambda a: a.reshape(SSM_CHUNK, g * p)
    return wb, flat(v_re), flat(v_im), flat(e_re), flat(e_im), wc


def _s5(u, tables, d_skip, w_glu_bf16, b_glu, layer, batch, seq):
    n = u.shape[0]
    wb, einv_re, einv_im, e_re, e_im, wc = tables
    steps = seq // SSM_ROWS
    nblk = N_STATE // SSM_BLOCK
    tab = (SSM_CHUNK, N_STATE)
    return pl.pallas_call(
        _s5_kernel,
        grid=(batch, steps),
        in_specs=[
            pl.BlockSpec((SSM_ROWS, SSM_WIDTH), lambda b, i: (b * steps + i, 0)),
            _resident(wb.shape),
            _resident(tab), _resident(tab), _resident(tab), _resident(tab),
            _resident(wc.shape),
            _layer_resident((1, SSM_WIDTH), layer),
            _layer_resident((SSM_WIDTH, SSM_WIDTH), layer),
            _layer_resident((1, SSM_WIDTH), layer),
        ],
        out_specs=pl.BlockSpec((SSM_ROWS, SSM_WIDTH), lambda b, i: (b * steps + i, 0)),
        out_shape=jax.ShapeDtypeStruct((n, SSM_WIDTH), BF16),
        scratch_shapes=[
            pltpu.VMEM((2, N_STATE), F32),
            pltpu.VMEM((nblk, SSM_ROWS, 2 * SSM_BLOCK), F32),
            pltpu.VMEM((nblk, SSM_ROWS, 2 * SSM_BLOCK), BF16),
            pltpu.VMEM((nblk, SSM_ROWS, 2 * SSM_BLOCK), F32),
            pltpu.VMEM((nblk, SSM_ROWS, 2 * SSM_BLOCK), BF16),
            pltpu.VMEM((SSM_ROWS, SSM_WIDTH), F32),
        ],
        compiler_params=_params("arbitrary", "arbitrary"),
        name="s5_scan",
    )(u, wb, einv_re, einv_im, e_re, e_im, wc,
      d_skip.reshape(-1, 1, SSM_WIDTH), w_glu_bf16, b_glu.reshape(-1, 1, SSM_WIDTH))


def _mixout_kernel(x_ref, att_ref, s5_ref, g_ref, mod_ref, wsb_ref, wssm_ref,
                   wout_ref, lng_ref, lnb_ref, o_ref, *, alpha):
    gate = mod_ref[2:3, :]
    slabs = [slice(r, r + MIX_SLAB) for r in range(0, ROW_TILE, MIX_SLAB)]
    branches = [(_dot(att_ref[rows, :], wsb_ref[...]), _dot(s5_ref[rows, :], wssm_ref[...]))
                for rows in slabs]
    merged = []
    for rows, (y_sb, y_ssm) in zip(slabs, branches):
        g_sb = g_ref[rows, :D_MODEL].astype(F32)
        g_ssm = g_ref[rows, D_MODEL:].astype(F32)
        merged.append((_sigmoid(g_sb) * y_sb + _sigmoid(g_ssm) * y_ssm).astype(BF16))
    ys = [_dot(m, wout_ref[...]) for m in merged]
    for rows, y in zip(slabs, ys):
        r = alpha * x_ref[rows, :] + (1.0 + gate) * y
        o_ref[rows, :] = _normalize(r) * lng_ref[...] + lnb_ref[...]


def _mix_out(x2d, att, s5, g, mod, w_sb_up, w_ssm_up, w_out, ln_g, ln_b, layer, seq, alpha):
    n = x2d.shape[0]
    tiles_per_seq = seq // ROW_TILE
    rows = lambda width: pl.BlockSpec((ROW_TILE, width), lambda i: (i, 0))
    return pl.pallas_call(
        functools.partial(_mixout_kernel, alpha=alpha),
        grid=(n // ROW_TILE,),
        in_specs=[
            rows(D_MODEL), rows(SB_WIDTH), rows(SSM_WIDTH), rows(2 * D_MODEL),
            _layer_mod(layer, tiles_per_seq),
            _layer_resident((SB_WIDTH, D_MODEL), layer),
            _layer_resident((SSM_WIDTH, D_MODEL), layer),
            _layer_resident((D_MODEL, D_MODEL), layer),
            _layer_resident((1, D_MODEL), layer),
            _layer_resident((1, D_MODEL), layer),
        ],
        out_specs=rows(D_MODEL),
        out_shape=jax.ShapeDtypeStruct((n, D_MODEL), F32),
        compiler_params=_params("arbitrary"),
        name="mix_out",
    )(x2d, att, s5, g, mod, w_sb_up, w_ssm_up, w_out,
      ln_g.reshape(-1, 1, D_MODEL), ln_b.reshape(-1, 1, D_MODEL))


def _ffn_kernel(x_ref, mod_ref, win_ref, wout_ref, lng_ref, lnb_ref, o_ref,
                h_ref, acc_ref, *, alpha):
    x = x_ref[...]
    shift = mod_ref[3:4, :]
    scale = mod_ref[4:5, :]
    gate_mod = mod_ref[5:6, :]
    h_ref[...] = (_normalize(x) * (1.0 + scale) + shift).astype(BF16)
    h = h_ref[...]
    for j in range(FFN_HIDDEN // FFN_CHUNK):
        cols = slice(j * FFN_CHUNK, (j + 1) * FFN_CHUNK)
        up_cols = slice(FFN_HIDDEN + j * FFN_CHUNK, FFN_HIDDEN + (j + 1) * FFN_CHUNK)
        gate = _dot(h, win_ref[:, cols])
        up = _dot(h, win_ref[:, up_cols])
        act = (gate * _sigmoid(gate) * up).astype(BF16)
        part = _dot(act, wout_ref[cols, :])
        if j == 0:
            acc_ref[...] = part
        else:
            acc_ref[...] += part
    r = alpha * x + (1.0 + gate_mod) * acc_ref[...]
    o_ref[...] = _normalize(r) * lng_ref[...] + lnb_ref[...]


def _ffn(x2d, mod, w_ffn_in, w_ffn_out, ln_g, ln_b, layer, seq, alpha):
    n = x2d.shape[0]
    tiles_per_seq = seq // ROW_TILE
    return pl.pallas_call(
        functools.partial(_ffn_kernel, alpha=alpha),
        grid=(n // ROW_TILE,),
        in_specs=[
            pl.BlockSpec((ROW_TILE, D_MODEL), lambda i: (i, 0)),
            _layer_mod(layer, tiles_per_seq),
            _layer_resident((D_MODEL, 2 * FFN_HIDDEN), layer),
            _layer_resident((FFN_HIDDEN, D_MODEL), layer),
            _layer_resident((1, D_MODEL), layer),
            _layer_resident((1, D_MODEL), layer),
        ],
        out_specs=pl.BlockSpec((ROW_TILE, D_MODEL), lambda i: (i, 0)),
        out_shape=jax.ShapeDtypeStruct((n, D_MODEL), F32),
        scratch_shapes=[pltpu.VMEM((ROW_TILE, D_MODEL), BF16),
                        pltpu.VMEM((ROW_TILE, D_MODEL), F32)],
        compiler_params=_params("arbitrary"),
        name="ffn",
    )(x2d, mod, w_ffn_in, w_ffn_out, ln_g.reshape(-1, 1, D_MODEL), ln_b.reshape(-1, 1, D_MODEL))


def kernel(x, c, w_ada, b_ada, w_in, w_sb_up, ssm_a_re, ssm_a_im, ssm_log_dt,
           ssm_b_re, ssm_b_im, ssm_c_re, ssm_c_im, ssm_d, w_glu, b_glu,
           w_ssm_up, w_out, ln1_g, ln1_b, w_ffn_in, w_ffn_out, ln2_g, ln2_b):
    batch, seq, d = x.shape
    depth = w_in.shape[0]
    assert d == D_MODEL and seq % max(ROW_TILE, ATT_TILE * ATT_BLOCKS_PER_STEP, SSM_ROWS) == 0
    alpha = (2 * depth) ** 0.25
    mod = _modulation(c, w_ada, b_ada)
    x2d = x.reshape(batch * seq, d)
    w_in, w_sb_up, w_ssm_up, w_out, w_glu, w_ffn_in, w_ffn_out = (
        w.astype(BF16) for w in (w_in, w_sb_up, w_ssm_up, w_out, w_glu, w_ffn_in, w_ffn_out))
    for l in range(depth):
        qkv, u, g = _in_projection(x2d, mod, w_in, l, seq)
        att = _attention(qkv, batch, seq)
        tables = _s5_tables(ssm_a_re[l], ssm_a_im[l], ssm_log_dt[l], ssm_b_re[l],
                            ssm_b_im[l], ssm_c_re[l], ssm_c_im[l])
        s5 = _s5(u, tables, ssm_d, w_glu, b_glu, l, batch, seq)
        x2d = _mix_out(x2d, att, s5, g, mod, w_sb_up, w_ssm_up, w_out, ln1_g, ln1_b,
                       l, seq, alpha)
        x2d = _ffn(x2d, mod, w_ffn_in, w_ffn_out, ln2_g, ln2_b, l, seq, alpha)
    return x2d.reshape(batch, seq, d)
```
